```python
import math
import jax, jax.numpy as jnp
from jax import lax
import numpy as np

D_MODEL = 2048
BATCH = 1
SEQ = 8192
DEPTH = 2
DEC_BATCH = 128
DEC_SEQ = 8
PAST_LEN = 2048
PAGE_SIZE = 128

BRANCH_W = D_MODEL // 2
N_BRANCH = 3
D_SG = BRANCH_W
SG_GROUPS = 8
SG_CHUNK = 128
H_SB = 8
HD_SB = BRANCH_W // H_SB
SB_BLOCK = 128
SB_BIAS_INIT = -6.0
H_DN = 8
DK_DN = D_MODEL // 16
DV_DN = BRANCH_W // H_DN
D_DN_K = H_DN * DK_DN
D_DN_CONV = 2 * D_DN_K + BRANCH_W
CONV_W = 4
DN_CHUNK = 64
IN_SIZES = (BRANCH_W, BRANCH_W, BRANCH_W,
            BRANCH_W, BRANCH_W, BRANCH_W, BRANCH_W,
            D_DN_K, D_DN_K, BRANCH_W, BRANCH_W, H_DN, H_DN,
            N_BRANCH * D_MODEL)
IN_SPLITS = tuple(int(s) for s in np.cumsum(IN_SIZES)[:-1])
D_IN = int(sum(IN_SIZES))
ALPHA = (2 * DEPTH) ** 0.25
BETA_INIT = (8 * DEPTH) ** -0.25
LN_EPS = 1e-5
RMS_EPS = 1e-6
POOL_NUM, POOL_DEN = 5, 4

kernel_name = 'hybrid_gated_sg_stickbreak_gdn_step'


def layer_norm(x, g, b):
    xf = x.astype(jnp.float32)
    mu = jnp.mean(xf, axis=-1, keepdims=True)
    var = jnp.mean(jnp.square(xf - mu), axis=-1, keepdims=True)
    return ((xf - mu) * lax.rsqrt(var + LN_EPS)).astype(x.dtype) * g + b


def rms_norm(x, g):
    xf = x.astype(jnp.float32)
    return (xf * lax.rsqrt(jnp.mean(jnp.square(xf), axis=-1, keepdims=True) + RMS_EPS)).astype(x.dtype) * g


def l2_norm(x):
    xf = x.astype(jnp.float32)
    return (xf * lax.rsqrt(jnp.sum(jnp.square(xf), axis=-1, keepdims=True) + RMS_EPS)).astype(x.dtype)


def chunk_spatial_gating(u, v, w_s, b_s):
    B, L, W = v.shape
    c = SG_CHUNK if L >= SG_CHUNK else L
    pad = (-L) % c
    n = (L + pad) // c
    vg = jnp.pad(v, ((0, 0), (0, pad), (0, 0))).reshape(B, n, c, SG_GROUPS, W // SG_GROUPS)
    w = jnp.where(jnp.tril(jnp.ones((c, c), bool)), w_s[:, :c, :c], 0.0)
    mixed = jnp.einsum('gts,bnsgd->bntgd', w, vg) + jnp.swapaxes(b_s[:, :c], 0, 1)[:, :, None]
    return u * mixed.reshape(B, n * c, W)[:, :L]


def stick_breaking(q, k, v, q_pos, k_pos, bias):
    z = (jnp.einsum('bqhd,bkhd->bhqk', q, k).astype(jnp.float32) * (HD_SB ** -0.5)
         + bias.astype(jnp.float32)[None, :, None, None])
    causal = k_pos[None, :] < q_pos[:, None]
    log_beta = jax.nn.log_sigmoid(z)
    log_1m = jnp.where(causal, log_beta - z, 0.0)
    after = lax.cumsum(log_1m, axis=log_1m.ndim - 1, reverse=True) - log_1m
    a = jnp.where(causal, jnp.exp(log_beta + after), 0.0)
    return jnp.einsum('bhqk,bkhd->bqhd', a.astype(v.dtype), v)


def stick_breaking_blocked(q, k, v, bias):
    B, L, H, d = q.shape
    nb = L // SB_BLOCK
    k_pos = jnp.arange(L)
    qb = jnp.swapaxes(q.reshape(B, nb, SB_BLOCK, H, d), 0, 1)

    def one_block(args):
        i, qi = args
        return stick_breaking(qi, k, v, i * SB_BLOCK + jnp.arange(SB_BLOCK), k_pos, bias)

    out = lax.map(one_block, (jnp.arange(nb), qb))
    return jnp.swapaxes(out, 0, 1).reshape(B, L, H, d)


def causal_conv(x, buf, w):
    L = x.shape[1]
    xp = jnp.concatenate([buf, x], axis=1)
    out = sum(xp[:, i:i + L] * w[i] for i in range(CONV_W))
    return jax.nn.silu(out), xp[:, xp.shape[1] - (CONV_W - 1):]


def gated_delta_rule(q, k, v, g, beta, s0):
    B, L, H, dk = q.shape
    dv = v.shape[-1]
    c = math.gcd(DN_CHUNK, L)
    n = L // c
    f = jnp.float32

    def blk(t):
        t = t.astype(f).reshape(B, n, c, H, *t.shape[3:])
        return jnp.moveaxis(t, (1, 3), (0, 2))

    qc, kc, vc, gc, bc = blk(q), blk(k), blk(v), blk(g), blk(beta)
    gcum = jnp.cumsum(gc, axis=-1)
    incl = jnp.tril(jnp.ones((c, c), bool))
    strict = jnp.tril(jnp.ones((c, c), bool), -1)
    diff = gcum[..., :, None] - gcum[..., None, :]
    decay = jnp.where(incl, jnp.exp(jnp.where(incl, diff, 0.0)), 0.0)
    kk = jnp.einsum('nbhid,nbhjd->nbhij', kc, kc)
    m = jnp.eye(c, dtype=f) + jnp.where(strict, bc[..., :, None] * kk * decay, 0.0)
    gam = jnp.exp(gcum)
    rhs = jnp.concatenate([bc[..., None] * vc, (bc * gam)[..., None] * kc], axis=-1)
    sol = lax.linalg.triangular_solve(m, rhs, left_side=True, lower=True, unit_diagonal=True)
    u_part, w_part = sol[..., :dv], sol[..., dv:]
    qk = jnp.einsum('nbhid,nbhjd->nbhij', qc, kc) * decay
    q_g = qc * gam[..., None]
    k_end = kc * jnp.exp(gcum[..., -1:] - gcum)[..., None]
    g_end = jnp.exp(gcum[..., -1])

    def step(s, xs):
        u_p, w_p, qk_n, qg_n, ke_n, ge_n = xs
        u = u_p - jnp.einsum('bhcd,bhde->bhce', w_p, s)
        o = jnp.einsum('bhcd,bhde->bhce', qg_n, s) + jnp.einsum('bhij,bhje->bhie', qk_n, u)
        s = ge_n[..., None, None] * s + jnp.einsum('bhcd,bhce->bhde', ke_n, u)
        return s, o

    s_fin, o = lax.scan(step, s0.astype(f), (u_part, w_part, qk, q_g, k_end, g_end))
    o = jnp.moveaxis(o, (0, 2), (1, 3)).reshape(B, L, H, dv)
    return o.astype(v.dtype), s_fin.astype(s0.dtype)


def mixer_layer(x, p, sb_past, conv_buf, s0):
    B, L, _ = x.shape
    proj = jnp.einsum('bld,dn->bln', x, p['w_in'])
    (a_u, a_v, a_z, b_q, b_k, b_v, b_z, c_q, c_k, c_v, c_z, c_a, c_b,
     gate_logits) = jnp.split(proj, IN_SPLITS, axis=-1)

    a_u = jax.nn.gelu(a_u)
    a_v = layer_norm(jax.nn.gelu(a_v), p['sg_ln_g'], p['sg_ln_b'])
    o_a = chunk_spatial_gating(a_u, a_v, p['sg_w'], p['sg_b']) * jax.nn.silu(a_z)

    q = b_q.reshape(B, L, H_SB, HD_SB)
    k = b_k.reshape(B, L, H_SB, HD_SB)
    v = b_v.reshape(B, L, H_SB, HD_SB)
    if sb_past is None:
        o_sb = stick_breaking_blocked(q, k, v, p['sb_bias'])
    else:
        past_k, past_v = sb_past
        past = past_k.shape[1]
        o_sb = stick_breaking(q, jnp.concatenate([past_k, k], axis=1), jnp.concatenate([past_v, v], axis=1),
                              past + jnp.arange(L), jnp.arange(past + L), p['sb_bias'])
    o_b = o_sb.reshape(B, L, BRANCH_W) * jax.nn.silu(b_z)

    qkv, conv_new = causal_conv(jnp.concatenate([c_q, c_k, c_v], axis=-1), conv_buf, p['dn_conv_w'])
    cq, ck, cv = jnp.split(qkv, (D_DN_K, 2 * D_DN_K), axis=-1)
    cq = l2_norm(cq.reshape(B, L, H_DN, DK_DN)) * (DK_DN ** -0.5)
    ck = l2_norm(ck.reshape(B, L, H_DN, DK_DN))
    cv = cv.reshape(B, L, H_DN, DV_DN)
    g = -jnp.exp(p['dn_a_log']) * jax.nn.softplus(c_a + p['dn_dt_bias'])
    o_dn, s_new = gated_delta_rule(cq, ck, cv, g, jax.nn.sigmoid(c_b), s0)
    o_c = rms_norm(o_dn, p['dn_norm_g']).reshape(B, L, BRANCH_W) * jax.nn.silu(c_z)

    branches = jnp.stack([o_a, o_b, o_c], axis=2)
    y = jnp.einsum('blnw,nwd->blnd', branches, p['w_branch'])
    gates = jax.nn.sigmoid(gate_logits.reshape(B, L, N_BRANCH, D_MODEL) + p['gate_b'])
    out = jnp.einsum('bld,de->ble', jnp.sum(gates * y, axis=2), p['w_out'])
    x_new = layer_norm(ALPHA * x + out, p['ln_g'], p['ln_b'])
    return x_new, k, v, conv_new, s_new, a_v


def setup_inputs(seed: int = 0) -> dict:
    key = jax.random.key(seed)
    ks = jax.random.split(key, 24)
    f32 = jnp.float32

    def nrm(k, shape, s=1.0):
        return jax.random.normal(k, shape, f32) * s

    n_pages = PAST_LEN // PAGE_SIZE
    n_pool = (DEC_BATCH * n_pages * POOL_NUM) // POOL_DEN
    page_table = jax.random.permutation(ks[4], n_pool)[:DEC_BATCH * n_pages].reshape(DEC_BATCH, n_pages).astype(jnp.int32)
    dt = jnp.exp(jax.random.uniform(ks[13], (DEPTH, H_DN), f32, math.log(1e-3), math.log(1e-1)))
    return {
        'x_prompt': nrm(ks[0], (BATCH, SEQ, D_MODEL)),
        'x_sample': nrm(ks[1], (DEC_BATCH, DEC_SEQ, D_MODEL)),
        'cache_sb_k': nrm(ks[2], (DEPTH, n_pool, PAGE_SIZE, H_SB, HD_SB)),
        'cache_sb_v': nrm(ks[3], (DEPTH, n_pool, PAGE_SIZE, H_SB, HD_SB)),
        'state_dn_conv': nrm(ks[5], (DEPTH, DEC_BATCH, CONV_W - 1, D_DN_CONV)),
        'state_dn_ssm': nrm(ks[6], (DEPTH, DEC_BATCH, H_DN, DK_DN, DV_DN), DK_DN ** -0.5),
        'page_table': page_table,
        'ln_in_g': 1.0 + nrm(ks[7], (D_MODEL,), 0.05),
        'ln_in_b': nrm(ks[8], (D_MODEL,), 0.02),
        'w_in': nrm(ks[9], (DEPTH, D_MODEL, D_IN), D_MODEL ** -0.5),
        'sg_ln_g': 1.0 + nrm(ks[10], (DEPTH, D_SG), 0.05),
        'sg_ln_b': nrm(ks[11], (DEPTH, D_SG), 0.02),
        'sg_w': nrm(ks[12], (DEPTH, SG_GROUPS, SG_CHUNK, SG_CHUNK), SG_CHUNK ** -0.5),
        'sg_b': 1.0 + nrm(ks[14], (DEPTH, SG_GROUPS, SG_CHUNK), 0.1),
        'sb_bias': SB_BIAS_INIT + nrm(ks[23], (DEPTH, H_SB), 0.1),
        'dn_conv_w': nrm(ks[15], (DEPTH, CONV_W, D_DN_CONV), CONV_W ** -0.5),
        'dn_a_log': jnp.log(jax.random.uniform(ks[16], (DEPTH, H_DN), f32, 1.0, 16.0)),
        'dn_dt_bias': dt + jnp.log(-jnp.expm1(-dt)),
        'dn_norm_g': 1.0 + nrm(ks[17], (DEPTH, DV_DN), 0.05),
        'gate_b': nrm(ks[18], (DEPTH, N_BRANCH, D_MODEL), 0.1),
        'w_branch': nrm(ks[19], (DEPTH, N_BRANCH, BRANCH_W, D_MODEL), BRANCH_W ** -0.5 * BETA_INIT),
        'w_out': nrm(ks[20], (DEPTH, D_MODEL, D_MODEL), D_MODEL ** -0.5 * BETA_INIT),
        'ln_g': 1.0 + nrm(ks[21], (DEPTH, D_MODEL), 0.05),
        'ln_b': nrm(ks[22], (DEPTH, D_MODEL), 0.02),
    }


def reference(x_prompt, x_sample, cache_sb_k, cache_sb_v, state_dn_conv, state_dn_ssm, page_table,
              ln_in_g, ln_in_b, w_in, sg_ln_g, sg_ln_b, sg_w, sg_b, sb_bias, dn_conv_w, dn_a_log, dn_dt_bias,
              dn_norm_g, gate_b, w_branch, w_out, ln_g, ln_b):
    bp, bs = x_prompt.shape[0], x_sample.shape[0]
    n_pages = page_table.shape[1]
    xp = layer_norm(x_prompt, ln_in_g, ln_in_b)
    xs = layer_norm(x_sample, ln_in_g, ln_in_b)
    kp_l, vp_l, ks_l, vs_l, cp_l, cs_l, sp_l, ss_l, sgv_l = [], [], [], [], [], [], [], [], []
    for l in range(DEPTH):
        p = {'w_in': w_in[l], 'sg_ln_g': sg_ln_g[l], 'sg_ln_b': sg_ln_b[l], 'sg_w': sg_w[l], 'sg_b': sg_b[l],
             'sb_bias': sb_bias[l],
             'dn_conv_w': dn_conv_w[l], 'dn_a_log': dn_a_log[l], 'dn_dt_bias': dn_dt_bias[l],
             'dn_norm_g': dn_norm_g[l], 'gate_b': gate_b[l], 'w_branch': w_branch[l], 'w_out': w_out[l],
             'ln_g': ln_g[l], 'ln_b': ln_b[l]}
        xp, kp, vp, cp, sp, _ = mixer_layer(
            xp, p, None,
            jnp.zeros((bp, CONV_W - 1, D_DN_CONV), xp.dtype),
            jnp.zeros((bp, H_DN, DK_DN, DV_DN), xp.dtype))
        past_k = cache_sb_k[l][page_table].reshape(bs, n_pages * PAGE_SIZE, H_SB, HD_SB)
        past_v = cache_sb_v[l][page_table].reshape(bs, n_pages * PAGE_SIZE, H_SB, HD_SB)
        xs, k_s, v_s, cs, ss, sgv = mixer_layer(xs, p, (past_k, past_v), state_dn_conv[l], state_dn_ssm[l])
        kp_l.append(kp); vp_l.append(vp); ks_l.append(k_s); vs_l.append(v_s)
        cp_l.append(cp); cs_l.append(cs); sp_l.append(sp); ss_l.append(ss); sgv_l.append(sgv)
    return (xp, xs,
            jnp.stack(kp_l), jnp.stack(vp_l), jnp.stack(ks_l), jnp.stack(vs_l),
            jnp.stack(cp_l), jnp.stack(cs_l), jnp.stack(sp_l), jnp.stack(ss_l), jnp.stack(sgv_l))
```

```python
import functools
import math

import jax
import jax.numpy as jnp
from jax import lax
from jax.experimental import pallas as pl
from jax.experimental.pallas import tpu as pltpu

F32 = jnp.float32
BF16 = jnp.bfloat16

VMEM_LIMIT_BYTES = 56 * 1024 * 1024
LANES = 128

D_MODEL = 2048
BRANCH_W = 1024
N_BRANCH = 3
SG_GROUPS = 8
SG_CHUNK = 128
H_SB = 8
HD_SB = 128
H_DN = 8
DK_DN = 128
DV_DN = 128
D_DN_CONV = 3 * BRANCH_W
CONV_W = 4
DN_CHUNK = 64
PAGE_SIZE = 128
LN_EPS = 1e-5
RMS_EPS = 1e-6
DEPTH = 2
ALPHA = (2 * DEPTH) ** 0.25

COL_A_U, COL_A_V, COL_A_Z = 0, 1024, 2048
COL_B_Q, COL_B_K, COL_B_V, COL_B_Z = 3072, 4096, 5120, 6144
COL_C_QKV, COL_C_Z = 7168, 10240
COL_GATE = 11264
N_MAIN = 17408
COL_SMALL = 11264


def _params(*sem):
    return pltpu.CompilerParams(dimension_semantics=sem, vmem_limit_bytes=VMEM_LIMIT_BYTES)


def _silu(x):
    return x * (1.0 / (1.0 + jnp.exp(-x)))


def _sigmoid(x):
    return 1.0 / (1.0 + jnp.exp(-x))


def _gelu(x):
    c = math.sqrt(2.0 / math.pi)
    return 0.5 * x * (1.0 + jnp.tanh(c * (x + 0.044715 * (x * x * x))))


def _layer_norm_rows(x, g, b):
    mu = jnp.mean(x, axis=-1, keepdims=True)
    xc = x - mu
    var = jnp.mean(xc * xc, axis=-1, keepdims=True)
    return xc * lax.rsqrt(var + LN_EPS) * g + b


def _ln_kernel(x_ref, g_ref, b_ref, o_ref):
    o_ref[...] = _layer_norm_rows(x_ref[...], g_ref[...], b_ref[...])


def _layer_norm(x, g, b, tm=512):
    m, d = x.shape
    tm = min(tm, m)
    return pl.pallas_call(
        _ln_kernel,
        grid=(m // tm,),
        in_specs=[pl.BlockSpec((tm, d), lambda i: (i, 0)),
                  pl.BlockSpec((1, d), lambda i: (0, 0)),
                  pl.BlockSpec((1, d), lambda i: (0, 0))],
        out_specs=pl.BlockSpec((tm, d), lambda i: (i, 0)),
        out_shape=jax.ShapeDtypeStruct((m, d), F32),
        compiler_params=_params("arbitrary"),
        name="ln_in",
    )(x, g.reshape(1, d), b.reshape(1, d))


def _mm_kernel(x_ref, w_ref, o_ref, xb_ref):
    @pl.when(pl.program_id(1) == 0)
    def _():
        xb_ref[...] = x_ref[...].astype(BF16)

    o_ref[...] = jnp.dot(xb_ref[...], w_ref[...], preferred_element_type=F32)


def _matmul(x, w, tm, tn, name):
    m, k = x.shape
    n = w.shape[1]
    tm = min(tm, m)
    return pl.pallas_call(
        _mm_kernel,
        grid=(m // tm, n // tn),
        in_specs=[pl.BlockSpec((tm, k), lambda i, j: (i, 0)),
                  pl.BlockSpec((k, tn), lambda i, j: (0, j))],
        out_specs=pl.BlockSpec((tm, tn), lambda i, j: (i, j)),
        out_shape=jax.ShapeDtypeStruct((m, n), F32),
        scratch_shapes=[pltpu.VMEM((tm, k), BF16)],
        compiler_params=_params("arbitrary", "arbitrary"),
        name=name,
    )(x, w)


def _sg_kernel(u_ref, v_ref, z_ref, w_ref, bias_ref, g_ref, b_ref, o_ref, vn_ref):
    u = _gelu(u_ref[...])
    vn = _layer_norm_rows(_gelu(v_ref[...]), g_ref[...], b_ref[...])
    vn_ref[...] = vn
    vb = vn.astype(BF16)
    rows = vn.shape[0]
    r = lax.broadcasted_iota(jnp.int32, (rows, rows), 0)
    c = lax.broadcasted_iota(jnp.int32, (rows, rows), 1)
    lower = c <= r
    parts = []
    for grp in range(SG_GROUPS):
        w = jnp.where(lower, w_ref[grp], 0.0).astype(BF16)
        parts.append(jnp.dot(w, vb[:, grp * LANES:(grp + 1) * LANES], preferred_element_type=F32))
    mixed = jnp.concatenate(parts, axis=-1) + bias_ref[...]
    o_ref[...] = u * mixed * _silu(z_ref[...])


def _spatial_gating(proj, w_mix, bias_full, ln_g, ln_b):
    m = proj.shape[0]
    tm = SG_CHUNK
    blk = lambda col: pl.BlockSpec((tm, BRANCH_W), lambda i, col=col: (i, col // BRANCH_W))
    return pl.pallas_call(
        _sg_kernel,
        grid=(m // tm,),
        in_specs=[blk(COL_A_U), blk(COL_A_V), blk(COL_A_Z),
                  pl.BlockSpec((SG_GROUPS, tm, tm), lambda i: (0, 0, 0)),
                  pl.BlockSpec((tm, BRANCH_W), lambda i: (0, 0)),
                  pl.BlockSpec((1, BRANCH_W), lambda i: (0, 0)),
                  pl.BlockSpec((1, BRANCH_W), lambda i: (0, 0))],
        out_specs=[pl.BlockSpec((tm, BRANCH_W), lambda i: (i, 0)),
                   pl.BlockSpec((tm, BRANCH_W), lambda i: (i, 0))],
        out_shape=[jax.ShapeDtypeStruct((m, BRANCH_W), F32),
                   jax.ShapeDtypeStruct((m, BRANCH_W), F32)],
        compiler_params=_params("arbitrary"),
        name="spatial_gating",
    )(proj, proj, proj, w_mix, bias_full, ln_g.reshape(1, -1), ln_b.reshape(1, -1))


def _merge_kernel(oa_ref, ob_ref, oc_ref, wb_ref, ga_ref, gb_ref, gc_ref, gbias_ref, o_ref, xb_ref):
    @pl.when(pl.program_id(1) == 0)
    def _():
        xb_ref[0] = oa_ref[...].astype(BF16)
        xb_ref[1] = ob_ref[...].astype(BF16)
        xb_ref[2] = oc_ref[...].astype(BF16)

    acc = None
    for n, g_ref in enumerate((ga_ref, gb_ref, gc_ref)):
        y = jnp.dot(xb_ref[n], wb_ref[n], preferred_element_type=F32)
        t = _sigmoid(g_ref[...] + gbias_ref[n:n + 1, :]) * y
        acc = t if acc is None else acc + t
    o_ref[...] = acc


def _merge(o_a, o_b, o_c, proj, w_branch, gate_b, tm=512, tn=512):
    m = o_a.shape[0]
    tm = min(tm, m)
    oblk = pl.BlockSpec((tm, BRANCH_W), lambda i, j: (i, 0))
    gblk = lambda n: pl.BlockSpec((tm, tn), lambda i, j, n=n: (i, (COL_GATE + n * D_MODEL) // tn + j))
    return pl.pallas_call(
        _merge_kernel,
        grid=(m // tm, D_MODEL // tn),
        in_specs=[oblk, oblk, oblk,
                  pl.BlockSpec((N_BRANCH, BRANCH_W, tn), lambda i, j: (0, 0, j)),
                  gblk(0), gblk(1), gblk(2),
                  pl.BlockSpec((N_BRANCH, tn), lambda i, j: (0, j))],
        out_specs=pl.BlockSpec((tm, tn), lambda i, j: (i, j)),
        out_shape=jax.ShapeDtypeStruct((m, D_MODEL), F32),
        scratch_shapes=[pltpu.VMEM((N_BRANCH, tm, BRANCH_W), BF16)],
        compiler_params=_params("arbitrary", "arbitrary"),
        name="merge",
    )(o_a, o_b, o_c, w_branch, proj, proj, proj, gate_b)


def _out_kernel(y_ref, w_ref, x_ref, g_ref, b_ref, o_ref):
    out = jnp.dot(y_ref[...].astype(BF16), w_ref[...], preferred_element_type=F32)
    o_ref[...] = _layer_norm_rows(ALPHA * x_ref[...] + out, g_ref[...], b_ref[...])


def _out_proj(y, w_out, x, ln_g, ln_b, tm=256):
    m, d = x.shape
    row = pl.BlockSpec((tm, d), lambda i: (i, 0))
    vec = pl.BlockSpec((1, d), lambda i: (0, 0))
    return pl.pallas_call(
        _out_kernel,
        grid=(m // tm,),
        in_specs=[row, pl.BlockSpec((d, d), lambda i: (0, 0)), row, vec, vec],
        out_specs=row,
        out_shape=jax.ShapeDtypeStruct((m, d), F32),
        compiler_params=_params("arbitrary"),
        name="out_proj",
    )(y, w_out, x, ln_g.reshape(1, d), ln_b.reshape(1, d))


SB_BK = 128
SB_SCALE = HD_SB ** -0.5


def _split2(x):
    hi = x.astype(BF16)
    lo = (x - hi.astype(F32)).astype(BF16)
    return hi, lo


def _split3(x):
    hi = x.astype(BF16)
    r = x - hi.astype(F32)
    mid = r.astype(BF16)
    lo = (r - mid.astype(F32)).astype(BF16)
    return hi, mid, lo


def _suffix_matrix(bk):
    j = lax.broadcasted_iota(jnp.int32, (bk, 2 * bk), 0)
    s = lax.broadcasted_iota(jnp.int32, (bk, 2 * bk), 1)
    return jnp.where((s >= bk) | (j > s), 1.0, 0.0).astype(BF16)


def _sb_block(z, carry, suffix, mask):
    rows, bk = z.shape
    e = jnp.log(1.0 + jnp.exp(-jnp.abs(z)))
    log_beta = jnp.minimum(z, 0.0) - e
    log_1m = -jnp.maximum(z, 0.0) - e
    if mask is not None:
        log_1m = jnp.where(mask, log_1m, 0.0)
    hi, lo = _split2(log_1m)
    r = jnp.dot(jnp.concatenate([hi, lo], axis=0), suffix, preferred_element_type=F32)
    r = r[:rows] + r[rows:]
    a = jnp.exp(log_beta + carry + r[:, :bk])
    if mask is not None:
        a = jnp.where(mask, a, 0.0)
    return a, carry + r[:, bk:]


def _sb_prompt_kernel(bias_ref, q_ref, k_ref, v_ref, zg_ref, o_ref, *, bq):
    h = pl.program_id(0)
    qi = pl.program_id(1)
    bk = SB_BK
    nsub = bq // bk
    bias = bias_ref[h]
    qb = q_ref[...].astype(BF16)
    suffix = _suffix_matrix(bk)
    q_pos = qi * bq + lax.broadcasted_iota(jnp.int32, (bq, bk), 0)
    k_off = lax.broadcasted_iota(jnp.int32, (bq, bk), 1)

    def step(j, carry, acc, masked):
        start = pl.multiple_of(j * bk, bk)
        kb = k_ref[pl.ds(start, bk), :].astype(BF16)
        vb = v_ref[pl.ds(start, bk), :].astype(BF16)
        s = lax.dot_general(qb, kb, (((1,), (1,)), ((), ())), preferred_element_type=F32)
        z = s * SB_SCALE + bias
        mask = (j * bk + k_off) < q_pos if masked else None
        a, carry = _sb_block(z, carry, suffix, mask)
        acc = acc + jnp.dot(a.astype(BF16), vb, preferred_element_type=F32)
        return carry, acc

    carry = jnp.zeros((bq, bk), F32)
    acc = jnp.zeros((bq, HD_SB), F32)
    for t in range(nsub - 1, -1, -1):
        carry, acc = step(qi * nsub + t, carry, acc, True)

    def body(t, ca):
        return step(qi * nsub - 1 - t, ca[0], ca[1], False)

    carry, acc = lax.fori_loop(0, qi * nsub, body, (carry, acc))
    o_ref[...] = acc * _silu(zg_ref[...])


def _sb_prompt(proj, sb_bias, bq=256):
    m = proj.shape[0]
    cb = lambda col: col // HD_SB
    return pl.pallas_call(
        functools.partial(_sb_prompt_kernel, bq=bq),
        grid=(H_SB, m // bq),
        in_specs=[pl.BlockSpec(memory_space=pltpu.SMEM),
                  pl.BlockSpec((bq, HD_SB), lambda h, i: (i, cb(COL_B_Q) + h)),
                  pl.BlockSpec((m, HD_SB), lambda h, i: (0, cb(COL_B_K) + h)),
                  pl.BlockSpec((m, HD_SB), lambda h, i: (0, cb(COL_B_V) + h)),
                  pl.BlockSpec((bq, HD_SB), lambda h, i: (i, cb(COL_B_Z) + h))],
        out_specs=pl.BlockSpec((bq, HD_SB), lambda h, i: (i, h)),
        out_shape=jax.ShapeDtypeStruct((m, BRANCH_W), F32),
        compiler_params=_params("arbitrary", "arbitrary"),
        name="sb_prompt",
    )(sb_bias, proj, proj, proj, proj)


def _sb_sample_kernel(pt_ref, bias_ref, q_ref, kn_ref, vn_ref, zg_ref, kc_ref, vc_ref, o_ref,
                      acc_ref, carry_ref, *, dec_seq, n_pages):
    del pt_ref
    j = pl.program_id(1)
    bk = PAGE_SIZE
    rows = H_SB * dec_seq
    suffix = _suffix_matrix(bk)

    def run(get_k, get_v, mask):
        zs = []
        for h in range(H_SB):
            qh = q_ref[:, h * HD_SB:(h + 1) * HD_SB].astype(BF16)
            s = lax.dot_general(qh, get_k(h).astype(BF16), (((1,), (1,)), ((), ())),
                                preferred_element_type=F32)
            zs.append(s * SB_SCALE + bias_ref[h])
        z = jnp.concatenate(zs, axis=0)
        a, carry = _sb_block(z, carry_ref[...], suffix, mask)
        carry_ref[...] = carry
        outs = []
        for h in range(H_SB):
            ah = a[h * dec_seq:(h + 1) * dec_seq].astype(BF16)
            outs.append(jnp.dot(ah, get_v(h).astype(BF16), preferred_element_type=F32))
        acc_ref[...] += jnp.concatenate(outs, axis=0)

    @pl.when(j == 0)
    def _():
        acc_ref[...] = jnp.zeros_like(acc_ref)
        carry_ref[...] = jnp.zeros_like(carry_ref)
        pad = jnp.zeros((bk - dec_seq, HD_SB), F32)
        new_k = lambda h: jnp.concatenate([kn_ref[:, h * HD_SB:(h + 1) * HD_SB], pad], axis=0)
        new_v = lambda h: jnp.concatenate([vn_ref[:, h * HD_SB:(h + 1) * HD_SB], pad], axis=0)
        t = lax.broadcasted_iota(jnp.int32, (rows, bk), 0) % dec_seq
        pos = lax.broadcasted_iota(jnp.int32, (rows, bk), 1)
        run(new_k, new_v, pos < t)

    @pl.when(j > 0)
    def _():
        page_k = lambda h: kc_ref[pl.ds(h, bk, stride=H_SB), :]
        page_v = lambda h: vc_ref[pl.ds(h, bk, stride=H_SB), :]
        run(page_k, page_v, None)

    @pl.when(j == n_pages)
    def _():
        for h in range(H_SB):
            sl = slice(h * HD_SB, (h + 1) * HD_SB)
            o_ref[:, sl] = acc_ref[h * dec_seq:(h + 1) * dec_seq, :] * _silu(zg_ref[:, sl])


def _sb_sample(proj, cache_k, cache_v, page_table, sb_bias, layer, dec_seq):
    m = proj.shape[0]
    nb = m // dec_seq
    n_pages = page_table.shape[1]
    cb = lambda col: col // BRANCH_W
    page = lambda b, j, pt: (layer, pt[b, n_pages - jnp.maximum(j, 1)], 0, 0)
    row = lambda col: pl.BlockSpec((dec_seq, BRANCH_W), lambda b, j, pt, col=col: (b, cb(col)))
    grid_spec = pltpu.PrefetchScalarGridSpec(
        num_scalar_prefetch=1,
        grid=(nb, n_pages + 1),
        in_specs=[pl.BlockSpec(memory_space=pltpu.SMEM),
                  row(COL_B_Q), row(COL_B_K), row(COL_B_V), row(COL_B_Z),
                  pl.BlockSpec((None, None, PAGE_SIZE * H_SB, HD_SB), page),
                  pl.BlockSpec((None, None, PAGE_SIZE * H_SB, HD_SB), page)],
        out_specs=pl.BlockSpec((dec_seq, BRANCH_W), lambda b, j, pt: (b, 0)),
        scratch_shapes=[pltpu.VMEM((H_SB * dec_seq, HD_SB), F32),
                        pltpu.VMEM((H_SB * dec_seq, PAGE_SIZE), F32)],
    )
    return pl.pallas_call(
        functools.partial(_sb_sample_kernel, dec_seq=dec_seq, n_pages=n_pages),
        grid_spec=grid_spec,
        out_shape=jax.ShapeDtypeStruct((m, BRANCH_W), F32),
        compiler_params=_params("arbitrary", "arbitrary"),
        name="sb_sample",
    )(page_table, sb_bias, proj, proj, proj, proj, cache_k, cache_v)


def _softplus(x):
    return jnp.maximum(x, 0.0) + jnp.log(1.0 + jnp.exp(-jnp.abs(x)))


def _dot_exact_rhs(a_f32, sel_bf16):
    hi, mid, lo = _split3(a_f32)
    d = lambda p: jnp.dot(p, sel_bf16, preferred_element_type=F32)
    return d(hi) + (d(mid) + d(lo))


def _dot_exact_lhs(sel_bf16, b_f32):
    hi, mid, lo = _split3(b_f32)
    d = lambda p: jnp.dot(sel_bf16, p, preferred_element_type=F32)
    return d(hi) + (d(mid) + d(lo))


def _dot_x3(a, b):
    ah, al = _split2(a)
    bh, bl = _split2(b)
    d = lambda x, y: jnp.dot(x, y, preferred_element_type=F32)
    return d(ah, bh) + (d(ah, bl) + d(al, bh))


def _dn_prep_tail(seg_inputs, w_ref, ps_ref, alog_ref, dtb_ref, eg_ref, eb_ref,
                  qn_ref, kn_ref, vc_ref, g_ref, beta_ref):
    outs = (qn_ref, kn_ref, vc_ref)
    for seg, (x0, x1, x2, x3) in enumerate(seg_inputs):
        sl = slice(seg * BRANCH_W, (seg + 1) * BRANCH_W)
        w = w_ref[:, sl]
        y = _silu(((x3 * w[0:1] + x2 * w[1:2]) + x1 * w[2:3]) + x0 * w[3:4])
        if seg < 2:
            scale = DK_DN ** -0.5 if seg == 0 else 1.0
            for h in range(H_DN):
                hs = slice(h * DK_DN, (h + 1) * DK_DN)
                yh = y[:, hs]
                ss = jnp.sum(yh * yh, axis=-1, keepdims=True)
                outs[seg][:, hs] = yh * lax.rsqrt(ss + RMS_EPS) * scale
        else:
            outs[seg][...] = y
    ps = ps_ref[...]
    g = -jnp.exp(alog_ref[...]) * _softplus(ps + dtb_ref[...])
    g_ref[...] = _dot_exact_rhs(g, eg_ref[...])
    beta_ref[...] = _dot_exact_rhs(_sigmoid(ps), eb_ref[...])


def _dn_prep_prompt_kernel(xq_ref, xk_ref, xv_ref, hq_ref, hk_ref, hv_ref, *rest):
    first = pl.program_id(0) == 0
    segs = []
    for x_ref, h_ref in ((xq_ref, hq_ref), (xk_ref, hk_ref), (xv_ref, hv_ref)):
        x = x_ref[...]
        halo = jnp.where(first, 0.0, h_ref[...])
        rid = lax.broadcasted_iota(jnp.int32, halo.shape, 0)
        shifted = []
        for k in (1, 2, 3):
            xr = pltpu.roll(x, k, 0)
            head = jnp.where(rid < k, pltpu.roll(halo, k, 0), xr[:8])
            shifted.append(jnp.concatenate([head, xr[8:]], axis=0))
        segs.append((x, *shifted))
    _dn_prep_tail(segs, *rest)


def _dn_prep_sample_kernel(xq_ref, xk_ref, xv_ref, *rest):
    shifted = rest[:9]
    segs = []
    for s, x_ref in enumerate((xq_ref, xk_ref, xv_ref)):
        segs.append((x_ref[...], shifted[s][...], shifted[3 + s][...], shifted[6 + s][...]))
    _dn_prep_tail(segs, *rest[9:])


def _dn_prep(proj, proj_small, shifted, conv_w, a_log, dt_bias, tm=256):
    m = proj.shape[0]
    cb = lambda col: col // BRANCH_W
    xblk = lambda s: pl.BlockSpec((tm, BRANCH_W), lambda i, s=s: (i, cb(COL_C_QKV) + s))
    in_specs = [xblk(0), xblk(1), xblk(2)]
    args = [proj, proj, proj]
    if shifted is None:
        hblk = lambda s: pl.BlockSpec(
            (8, BRANCH_W), lambda i, s=s: (jnp.maximum(i * (tm // 8) - 1, 0), cb(COL_C_QKV) + s))
        in_specs += [hblk(0), hblk(1), hblk(2)]
        args += [proj, proj, proj]
        body = _dn_prep_prompt_kernel
    else:
        for arr in shifted:
            for s in range(3):
                in_specs.append(pl.BlockSpec((tm, BRANCH_W), lambda i, s=s: (i, s)))
                args.append(arr)
        body = _dn_prep_sample_kernel
    lane = lax.broadcasted_iota(jnp.int32, (LANES, BRANCH_W), 0)
    head = lax.broadcasted_iota(jnp.int32, (LANES, BRANCH_W), 1) // DV_DN
    e_g = (lane == head).astype(BF16)
    e_b = (lane == head + H_DN).astype(BF16)
    pad = lambda v, off: jnp.zeros((1, LANES), F32).at[0, off:off + H_DN].set(v)
    const = lambda shape: pl.BlockSpec(shape, lambda i: (0, 0))
    in_specs += [const((CONV_W, D_DN_CONV)), pl.BlockSpec((tm, LANES), lambda i: (i, 0)),
                 const((1, LANES)), const((1, LANES)), const((LANES, BRANCH_W)), const((LANES, BRANCH_W))]
    args += [conv_w, proj_small, pad(a_log, 0), pad(dt_bias, 0), e_g, e_b]
    out = pl.BlockSpec((tm, BRANCH_W), lambda i: (i, 0))
    return pl.pallas_call(
        body,
        grid=(m // tm,),
        in_specs=in_specs,
        out_specs=[out] * 5,
        out_shape=[jax.ShapeDtypeStruct((m, BRANCH_W), F32)] * 5,
        compiler_params=_params("arbitrary"),
        name="dn_prep",
    )(*args)


def _dn_masks(c, block):
    i = lax.broadcasted_iota(jnp.int32, (c, c), 0)
    j = lax.broadcasted_iota(jnp.int32, (c, c), 1)
    same = (i // block) == (j // block)
    return dict(incl=same & (i >= j), strict=same & (i > j), incl_t=same & (i <= j), same=same,
                eye=(i == j))


def _dn_chunk_core(q, k, v, g_rep, beta_rep, masks, n_double):
    c = q.shape[0]
    onef = lambda m: jnp.where(m, 1.0, 0.0).astype(BF16)
    gcum = _dot_exact_lhs(onef(masks["incl"]), g_rep)
    glast = _dot_exact_lhs(onef(masks["same"]), g_rep)
    g_row = _dot_exact_lhs(jnp.ones((c, c), BF16),
                           jnp.where(masks["incl_t"], g_rep[:, :c], 0.0))
    incl = masks["incl"]
    decay = jnp.where(incl, jnp.exp(jnp.where(incl, gcum[:, :c] - g_row, 0.0)), 0.0)
    kb = k.astype(BF16)
    kk = lax.dot_general(kb, kb, (((1,), (1,)), ((), ())), preferred_element_type=F32)
    a_mat = jnp.where(masks["strict"], beta_rep[:, :c] * kk * decay, 0.0)
    eye = jnp.where(masks["eye"], 1.0, 0.0)
    p = -a_mat
    x = eye + p
    for _ in range(n_double):
        p = _dot_x3(p, p)
        x = x + _dot_x3(x, p)
    gam = jnp.exp(gcum)
    u_part = _dot_x3(x, beta_rep * v)
    w_part = _dot_x3(x, (beta_rep * gam) * k)
    qk = lax.dot_general(q.astype(BF16), kb, (((1,), (1,)), ((), ())), preferred_element_type=F32) * decay
    q_g = q * gam
    k_end = k * jnp.exp(glast - gcum)
    return u_part, w_part, qk, q_g, k_end, jnp.exp(glast)


def _gated_rms(o, norm_g, z):
    ms = jnp.mean(o * o, axis=-1, keepdims=True)
    return o * lax.rsqrt(ms + RMS_EPS) * norm_g * _silu(z)


def _dn_prompt_kernel(q_ref, k_ref, v_ref, g_ref, b_ref, z_ref, ng_ref, o_ref, sfin_ref, s_ref):
    n = pl.program_id(1)

    @pl.when(n == 0)
    def _():
        s_ref[...] = jnp.zeros_like(s_ref)

    c = q_ref.shape[0]
    masks = _dn_masks(c, c)
    u_p, w_p, qk, q_g, k_end, g_end = _dn_chunk_core(
        q_ref[...], k_ref[...], v_ref[...], g_ref[...], b_ref[...], masks, int(math.log2(c)) - 1)
    s = s_ref[...]
    sb = s.astype(BF16)
    u = u_p - jnp.dot(w_p.astype(BF16), sb, preferred_element_type=F32)
    ub = u.astype(BF16)
    o = (jnp.dot(q_g.astype(BF16), sb, preferred_element_type=F32)
         + jnp.dot(qk.astype(BF16), ub, preferred_element_type=F32))
    s_new = g_end[0:1, :] * s + lax.dot_general(k_end.astype(BF16), ub, (((0,), (0,)), ((), ())),
                                                preferred_element_type=F32)
    s_ref[...] = s_new
    o_ref[...] = _gated_rms(o, ng_ref[...], z_ref[...])

    @pl.when(n == pl.num_programs(1) - 1)
    def _():
        sfin_ref[...] = s_new


def _dn_prompt(qn, kn, vc, g_rep, beta_rep, proj, norm_g):
    m = qn.shape[0]
    c = DN_CHUNK
    blk = pl.BlockSpec((c, DV_DN), lambda h, n: (n, h))
    zblk = pl.BlockSpec((c, DV_DN), lambda h, n: (n, COL_C_Z // DV_DN + h))
    return pl.pallas_call(
        _dn_prompt_kernel,
        grid=(H_DN, m // c),
        in_specs=[blk, blk, blk, blk, blk, zblk, pl.BlockSpec((1, DV_DN), lambda h, n: (0, 0))],
        out_specs=[blk, pl.BlockSpec((None, DK_DN, DV_DN), lambda h, n: (h, 0, 0))],
        out_shape=[jax.ShapeDtypeStruct((m, BRANCH_W), F32),
                   jax.ShapeDtypeStruct((H_DN, DK_DN, DV_DN), F32)],
        scratch_shapes=[pltpu.VMEM((DK_DN, DV_DN), F32)],
        compiler_params=_params("arbitrary", "arbitrary"),
        name="dn_prompt",
    )(qn, kn, vc, g_rep, beta_rep, proj, norm_g.reshape(1, DV_DN))


def _dn_sample_kernel(q_ref, k_ref, v_ref, g_ref, b_ref, z_ref, ng_ref, s0_ref, o_ref, snew_ref, *, dec_seq):
    t = dec_seq
    rows = H_DN * t
    stack = lambda ref: jnp.concatenate([ref[:, h * DV_DN:(h + 1) * DV_DN] for h in range(H_DN)], axis=0)
    masks = _dn_masks(rows, t)
    u_p, w_p, qk, q_g, k_end, g_end = _dn_chunk_core(
        stack(q_ref), stack(k_ref), stack(v_ref), stack(g_ref), stack(b_ref), masks,
        max(int(math.log2(t)) - 1, 0))
    us, os_ = [], []
    for h in range(H_DN):
        r = slice(h * t, (h + 1) * t)
        sb = s0_ref[h].astype(BF16)
        us.append(u_p[r] - jnp.dot(w_p[r].astype(BF16), sb, preferred_element_type=F32))
        os_.append(jnp.dot(q_g[r].astype(BF16), sb, preferred_element_type=F32))
    u = jnp.concatenate(us, axis=0)
    ub = u.astype(BF16)
    o = jnp.concatenate(os_, axis=0) + jnp.dot(qk.astype(BF16), ub, preferred_element_type=F32)
    rid = lax.broadcasted_iota(jnp.int32, (rows, DK_DN), 0) // t
    for h in range(H_DN):
        r = slice(h * t, (h + 1) * t)
        kh = jnp.where(rid == h, k_end, 0.0).astype(BF16)
        upd = lax.dot_general(kh, ub, (((0,), (0,)), ((), ())), preferred_element_type=F32)
        snew_ref[h] = g_end[h * t:h * t + 1, :] * s0_ref[h] + upd
        hs = slice(h * DV_DN, (h + 1) * DV_DN)
        o_ref[:, hs] = _gated_rms(o[r], ng_ref[...], z_ref[:, hs])


def _dn_sample(qn, kn, vc, g_rep, beta_rep, proj, norm_g, s0, layer, dec_seq):
    m = qn.shape[0]
    nb = m // dec_seq
    blk = pl.BlockSpec((dec_seq, BRANCH_W), lambda b: (b, 0))
    zblk = pl.BlockSpec((dec_seq, BRANCH_W), lambda b: (b, COL_C_Z // BRANCH_W))
    return pl.pallas_call(
        functools.partial(_dn_sample_kernel, dec_seq=dec_seq),
        grid=(nb,),
        in_specs=[blk, blk, blk, blk, blk, zblk, pl.BlockSpec((1, DV_DN), lambda b: (0, 0)),
                  pl.BlockSpec((None, None, H_DN, DK_DN, DV_DN), lambda b: (layer, b, 0, 0, 0))],
        out_specs=[blk, pl.BlockSpec((None, H_DN, DK_DN, DV_DN), lambda b: (b, 0, 0, 0))],
        out_shape=[jax.ShapeDtypeStruct((m, BRANCH_W), F32),
                   jax.ShapeDtypeStruct((nb, H_DN, DK_DN, DV_DN), F32)],
        compiler_params=_params("arbitrary"),
        name="dn_sample",
    )(qn, kn, vc, g_rep, beta_rep, proj, norm_g.reshape(1, DV_DN), s0)


def _mixer_layer(x, p, sample, tm):
    m = x.shape[0]
    proj = _matmul(x, p["w_main"], tm, 1024, "proj_main")
    proj_small = _matmul(x, p["w_small"], tm, LANES, "proj_small")

    if sample is None:
        w_mix, bias_full = p["sg_w"], p["sg_bias_prompt"]
    else:
        w_mix, bias_full = p["sg_w_sample"], p["sg_bias_sample"]
    o_a, sg_v = _spatial_gating(proj, w_mix, bias_full, p["sg_ln_g"], p["sg_ln_b"])

    conv_in = proj[:, COL_C_QKV:COL_C_QKV + D_DN_CONV]
    if sample is None:
        o_b = _sb_prompt(proj, p["sb_bias"])
        qn, kn, vc, g_rep, beta_rep = _dn_prep(proj, proj_small, None, p["dn_conv_w"], p["dn_a_log"],
                                               p["dn_dt_bias"])
        o_c, s_new = _dn_prompt(qn, kn, vc, g_rep, beta_rep, proj, p["dn_norm_g"])
        conv_new = conv_in[m - (CONV_W - 1):][None]
        s_new = s_new[None]
    else:
        t = sample["dec_seq"]
        nb = m // t
        o_b = _sb_sample(proj, sample["cache_k"], sample["cache_v"], sample["page_table"], p["sb_bias"],
                         sample["layer"], t)
        xp = jnp.concatenate([sample["conv"], conv_in.reshape(nb, t, D_DN_CONV)], axis=1)
        shifted = [xp[:, CONV_W - 1 - k:CONV_W - 1 - k + t].reshape(m, D_DN_CONV) for k in (1, 2, 3)]
        qn, kn, vc, g_rep, beta_rep = _dn_prep(proj, proj_small, shifted, p["dn_conv_w"], p["dn_a_log"],
                                               p["dn_dt_bias"])
        o_c, s_new = _dn_sample(qn, kn, vc, g_rep, beta_rep, proj, p["dn_norm_g"], sample["ssm"],
                                sample["layer"], t)
        conv_new = xp[:, t:]

    y = _merge(o_a, o_b, o_c, proj, p["w_branch"], p["gate_b"])
    x_new = _out_proj(y, p["w_out"], x, p["ln_g"], p["ln_b"])
    k_rows = proj[:, COL_B_K:COL_B_K + BRANCH_W]
    v_rows = proj[:, COL_B_V:COL_B_V + BRANCH_W]
    return x_new, k_rows, v_rows, conv_new, s_new, sg_v


def kernel(x_prompt, x_sample, cache_sb_k, cache_sb_v, state_dn_conv, state_dn_ssm, page_table, ln_in_g, ln_in_b,
           w_in, sg_ln_g, sg_ln_b, sg_w, sg_b, sb_bias, dn_conv_w, dn_a_log, dn_dt_bias, dn_norm_g, gate_b,
           w_branch, w_out, ln_g, ln_b):
    bp, seq, d = x_prompt.shape
    bs, dec_seq, _ = x_sample.shape
    assert bp == 1, "prompt rows are treated as one sequence"
    depth = w_in.shape[0]
    n_pool = cache_sb_k.shape[1]

    xp = _layer_norm(x_prompt.reshape(bp * seq, d), ln_in_g, ln_in_b)
    xs = _layer_norm(x_sample.reshape(bs * dec_seq, d), ln_in_g, ln_in_b)
    cache_k = cache_sb_k.reshape(depth, n_pool, PAGE_SIZE * H_SB, HD_SB)
    cache_v = cache_sb_v.reshape(depth, n_pool, PAGE_SIZE * H_SB, HD_SB)

    reps = SG_CHUNK // dec_seq
    eye = jnp.eye(reps, dtype=F32)
    outs = [[] for _ in range(9)]
    for l in range(depth):
        small = w_in[l][:, COL_SMALL:COL_SMALL + 2 * H_DN]
        sg_small = jnp.tril(sg_w[l][:, :dec_seq, :dec_seq])
        p = {
            "w_main": jnp.concatenate([w_in[l][:, :COL_SMALL], w_in[l][:, COL_SMALL + 2 * H_DN:]],
                                      axis=1).astype(BF16),
            "w_small": jnp.pad(small, ((0, 0), (0, LANES - 2 * H_DN))).astype(BF16),
            "sg_w": sg_w[l],
            "sg_bias_prompt": jnp.repeat(sg_b[l].T, BRANCH_W // SG_GROUPS, axis=1),
            "sg_w_sample": jax.vmap(lambda w: jnp.kron(eye, w))(sg_small),
            "sg_bias_sample": jnp.repeat(jnp.tile(sg_b[l][:, :dec_seq].T, (reps, 1)),
                                         BRANCH_W // SG_GROUPS, axis=1),
            "sg_ln_g": sg_ln_g[l], "sg_ln_b": sg_ln_b[l], "sb_bias": sb_bias[l],
            "dn_conv_w": dn_conv_w[l], "dn_a_log": dn_a_log[l], "dn_dt_bias": dn_dt_bias[l],
            "dn_norm_g": dn_norm_g[l], "gate_b": gate_b[l],
            "w_branch": w_branch[l].astype(BF16), "w_out": w_out[l].astype(BF16),
            "ln_g": ln_g[l], "ln_b": ln_b[l],
        }
        xp, kp, vp, cp, sp, _ = _mixer_layer(xp, p, None, tm=1024)
        sample = {"cache_k": cache_k, "cache_v": cache_v, "page_table": page_table,
                  "conv": state_dn_conv[l], "ssm": state_dn_ssm, "layer": l, "dec_seq": dec_seq}
        xs, ks, vs, cs, ss, sgv = _mixer_layer(xs, p, sample, tm=1024)
        hd = (H_SB, HD_SB)
        for lst, val in zip(outs, (kp.reshape(bp, seq, *hd), vp.reshape(bp, seq, *hd),
                                   ks.reshape(bs, dec_seq, *hd), vs.reshape(bs, dec_seq, *hd),
                                   cp, cs, sp, ss, sgv.reshape(bs, dec_seq, BRANCH_W))):
            lst.append(val)
    return (xp.reshape(bp, seq, d), xs.reshape(bs, dec_seq, d), *[jnp.stack(o) for o in outs])
```

```python
import functools
import math

import jax
import jax.numpy as jnp
from jax import lax
from jax.experimental import pallas as pl
from jax.experimental.pallas import tpu as pltpu

F32 = jnp.float32
BF16 = jnp.bfloat16

VMEM_LIMIT_BYTES = 56 * 1024 * 1024
LANES = 128

D_MODEL = 2048
BRANCH_W = 1024
N_BRANCH = 3
SG_GROUPS = 8
SG_CHUNK = 128
H_SB = 8
HD_SB = 128
H_DN = 8
DK_DN = 128
DV_DN = 128
D_DN_CONV = 3 * BRANCH_W
CONV_W = 4
DN_CHUNK = 64
PAGE_SIZE = 128
LN_EPS = 1e-5
RMS_EPS = 1e-6
DEPTH = 2
ALPHA = (2 * DEPTH) ** 0.25

COL_A_U, COL_A_V, COL_A_Z = 0, 1024, 2048
COL_B_Q, COL_B_K, COL_B_V, COL_B_Z = 3072, 4096, 5120, 6144
COL_C_QKV, COL_C_Z = 7168, 10240
COL_GATE = 11264
N_MAIN = 17408
COL_SMALL = 11264


def _params(*sem):
    return pltpu.CompilerParams(dimension_semantics=sem, vmem_limit_bytes=VMEM_LIMIT_BYTES)


def _silu(x):
    return x * (1.0 / (1.0 + jnp.exp(-x)))


def _sigmoid(x):
    return 1.0 / (1.0 + jnp.exp(-x))


def _gelu(x):
    c = math.sqrt(2.0 / math.pi)
    return 0.5 * x * (1.0 + jnp.tanh(c * (x + 0.044715 * (x * x * x))))


def _layer_norm_rows(x, g, b):
    mu = jnp.mean(x, axis=-1, keepdims=True)
    xc = x - mu
    var = jnp.mean(xc * xc, axis=-1, keepdims=True)
    return xc * lax.rsqrt(var + LN_EPS) * g + b


def _ln_kernel(x_ref, g_ref, b_ref, o_ref):
    o_ref[...] = _layer_norm_rows(x_ref[...], g_ref[...], b_ref[...])


def _layer_norm(x, g, b, tm=512):
    m, d = x.shape
    tm = min(tm, m)
    return pl.pallas_call(
        _ln_kernel,
        grid=(m // tm,),
        in_specs=[pl.BlockSpec((tm, d), lambda i: (i, 0)),
                  pl.BlockSpec((1, d), lambda i: (0, 0)),
                  pl.BlockSpec((1, d), lambda i: (0, 0))],
        out_specs=pl.BlockSpec((tm, d), lambda i: (i, 0)),
        out_shape=jax.ShapeDtypeStruct((m, d), F32),
        compiler_params=_params("arbitrary"),
        name="ln_in",
    )(x, g.reshape(1, d), b.reshape(1, d))


def _mm_kernel(x_ref, w_ref, o_ref, xb_ref):
    @pl.when(pl.program_id(1) == 0)
    def _():
        xb_ref[...] = x_ref[...].astype(BF16)

    o_ref[...] = jnp.dot(xb_ref[...], w_ref[...], preferred_element_type=F32)


def _matmul(x, w, tm, tn, name):
    m, k = x.shape
    n = w.shape[1]
    tm = min(tm, m)
    return pl.pallas_call(
        _mm_kernel,
        grid=(m // tm, n // tn),
        in_specs=[pl.BlockSpec((tm, k), lambda i, j: (i, 0)),
                  pl.BlockSpec((k, tn), lambda i, j: (0, j))],
        out_specs=pl.BlockSpec((tm, tn), lambda i, j: (i, j)),
        out_shape=jax.ShapeDtypeStruct((m, n), F32),
        scratch_shapes=[pltpu.VMEM((tm, k), BF16)],
        compiler_params=_params("arbitrary", "arbitrary"),
        name=name,
    )(x, w)


def _sg_kernel(u_ref, v_ref, z_ref, w_ref, bias_ref, g_ref, b_ref, o_ref, vn_ref):
    u = _gelu(u_ref[...])
    vn = _layer_norm_rows(_gelu(v_ref[...]), g_ref[...], b_ref[...])
    vn_ref[...] = vn
    vb = vn.astype(BF16)
    rows = vn.shape[0]
    r = lax.broadcasted_iota(jnp.int32, (rows, rows), 0)
    c = lax.broadcasted_iota(jnp.int32, (rows, rows), 1)
    lower = c <= r
    parts = []
    for grp in range(SG_GROUPS):
        w = jnp.where(lower, w_ref[grp], 0.0).astype(BF16)
        parts.append(jnp.dot(w, vb[:, grp * LANES:(grp + 1) * LANES], preferred_element_type=F32))
    mixed = jnp.concatenate(parts, axis=-1) + bias_ref[...]
    o_ref[...] = u * mixed * _silu(z_ref[...])


def _spatial_gating(proj, w_mix, bias_full, ln_g, ln_b):
    m = proj.shape[0]
    tm = SG_CHUNK
    blk = lambda col: pl.BlockSpec((tm, BRANCH_W), lambda i, col=col: (i, col // BRANCH_W))
    return pl.pallas_call(
        _sg_kernel,
        grid=(m // tm,),
        in_specs=[blk(COL_A_U), blk(COL_A_V), blk(COL_A_Z),
                  pl.BlockSpec((SG_GROUPS, tm, tm), lambda i: (0, 0, 0)),
                  pl.BlockSpec((tm, BRANCH_W), lambda i: (0, 0)),
                  pl.BlockSpec((1, BRANCH_W), lambda i: (0, 0)),
                  pl.BlockSpec((1, BRANCH_W), lambda i: (0, 0))],
        out_specs=[pl.BlockSpec((tm, BRANCH_W), lambda i: (i, 0)),
                   pl.BlockSpec((tm, BRANCH_W), lambda i: (i, 0))],
        out_shape=[jax.ShapeDtypeStruct((m, BRANCH_W), F32),
                   jax.ShapeDtypeStruct((m, BRANCH_W), F32)],
        compiler_params=_params("arbitrary"),
        name="spatial_gating",
    )(proj, proj, proj, w_mix, bias_full, ln_g.reshape(1, -1), ln_b.reshape(1, -1))


def _merge_kernel(oa_ref, ob_ref, oc_ref, wb_ref, ga_ref, gb_ref, gc_ref, gbias_ref, o_ref, xb_ref):
    @pl.when(pl.program_id(1) == 0)
    def _():
        xb_ref[0] = oa_ref[...].astype(BF16)
        xb_ref[1] = ob_ref[...].astype(BF16)
        xb_ref[2] = oc_ref[...].astype(BF16)

    acc = None
    for n, g_ref in enumerate((ga_ref, gb_ref, gc_ref)):
        y = jnp.dot(xb_ref[n], wb_ref[n], preferred_element_type=F32)
        t = _sigmoid(g_ref[...] + gbias_ref[n:n + 1, :]) * y
        acc = t if acc is None else acc + t
    o_ref[...] = acc


def _merge(o_a, o_b, o_c, proj, w_branch, gate_b, tm=512, tn=512):
    m = o_a.shape[0]
    tm = min(tm, m)
    oblk = pl.BlockSpec((tm, BRANCH_W), lambda i, j: (i, 0))
    gblk = lambda n: pl.BlockSpec((tm, tn), lambda i, j, n=n: (i, (COL_GATE + n * D_MODEL) // tn + j))
    return pl.pallas_call(
        _merge_kernel,
        grid=(m // tm, D_MODEL // tn),
        in_specs=[oblk, oblk, oblk,
                  pl.BlockSpec((N_BRANCH, BRANCH_W, tn), lambda i, j: (0, 0, j)),
                  gblk(0), gblk(1), gblk(2),
                  pl.BlockSpec((N_BRANCH, tn), lambda i, j: (0, j))],
        out_specs=pl.BlockSpec((tm, tn), lambda i, j: (i, j)),
        out_shape=jax.ShapeDtypeStruct((m, D_MODEL), F32),
        scratch_shapes=[pltpu.VMEM((N_BRANCH, tm, BRANCH_W), BF16)],
        compiler_params=_params("arbitrary", "arbitrary"),
        name="merge",
    )(o_a, o_b, o_c, w_branch, proj, proj, proj, gate_b)


def _out_kernel(y_ref, w_ref, x_ref, g_ref, b_ref, o_ref):
    out = jnp.dot(y_ref[...].astype(BF16), w_ref[...], preferred_element_type=F32)
    o_ref[...] = _layer_norm_rows(ALPHA * x_ref[...] + out, g_ref[...], b_ref[...])


def _out_proj(y, w_out, x, ln_g, ln_b, tm=256):
    m, d = x.shape
    row = pl.BlockSpec((tm, d), lambda i: (i, 0))
    vec = pl.BlockSpec((1, d), lambda i: (0, 0))
    return pl.pallas_call(
        _out_kernel,
        grid=(m // tm,),
        in_specs=[row, pl.BlockSpec((d, d), lambda i: (0, 0)), row, vec, vec],
        out_specs=row,
        out_shape=jax.ShapeDtypeStruct((m, d), F32),
        compiler_params=_params("arbitrary"),
        name="out_proj",
    )(y, w_out, x, ln_g.reshape(1, d), ln_b.reshape(1, d))


SB_SCALE = HD_SB ** -0.5


def _split2(x):
    hi = x.astype(BF16)
    lo = (x - hi.astype(F32)).astype(BF16)
    return hi, lo


def _split3(x):
    hi = x.astype(BF16)
    r = x - hi.astype(F32)
    mid = r.astype(BF16)
    lo = (r - mid.astype(F32)).astype(BF16)
    return hi, mid, lo


def _sb_weights(zs, carry, suffix, masks):
    log_betas, log_1ms, pieces = [], [], []
    for z, mask in zip(zs, masks):
        e = jnp.log(1.0 + jnp.exp(-jnp.abs(z)))
        log_1m = -jnp.maximum(z, 0.0) - e
        log_betas.append(log_1m + z)
        if mask is not None:
            log_1m = jnp.where(mask, log_1m, 0.0)
        log_1ms.append(log_1m)
        pieces.append(_split2(log_1m))
    withins = [jnp.dot(hi, suffix, preferred_element_type=F32) + jnp.dot(lo, suffix, preferred_element_type=F32)
               for hi, lo in pieces]
    weights = []
    for log_beta, log_1m, within, mask in zip(log_betas, log_1ms, withins, masks):
        a = jnp.exp(log_beta + (carry + within))
        weights.append(a if mask is None else jnp.where(mask, a, 0.0))
        carry = carry + jnp.sum(log_1m, axis=-1, keepdims=True)
    return weights, carry


def _sb_prompt_kernel(bias_ref, q_ref, k_ref, v_ref, zg_ref, o_ref, kb_ref, vb_ref, *, bq, bk, unroll):
    h = pl.program_id(0)
    qi = pl.program_id(1)
    nsub = bq // bk

    @pl.when(qi == 0)
    def _():
        kb_ref[...] = k_ref[...].astype(BF16)
        vb_ref[...] = v_ref[...].astype(BF16)

    bias = bias_ref[h]
    qb = (q_ref[...] * SB_SCALE).astype(BF16)
    jj = lax.broadcasted_iota(jnp.int32, (bk, bk), 0)
    ss = lax.broadcasted_iota(jnp.int32, (bk, bk), 1)
    suffix = jnp.where(jj > ss, 1.0, 0.0).astype(BF16)
    q_pos = qi * bq + lax.broadcasted_iota(jnp.int32, (bq, bk), 0)
    k_off = lax.broadcasted_iota(jnp.int32, (bq, bk), 1)

    def steps(js, carry, acc, masked):
        starts = [pl.multiple_of(j * bk, bk) for j in js]
        zs = [lax.dot_general(qb, kb_ref[pl.ds(st, bk), :], (((1,), (1,)), ((), ())),
                              preferred_element_type=F32) + bias for st in starts]
        masks = [(j * bk + k_off) < q_pos if masked else None for j in js]
        weights, carry = _sb_weights(zs, carry, suffix, masks)
        for a, st in zip(weights, starts):
            acc = acc + jnp.dot(a.astype(BF16), vb_ref[pl.ds(st, bk), :], preferred_element_type=F32)
        return carry, acc

    carry = jnp.zeros((bq, 1), F32)
    acc = jnp.zeros((bq, HD_SB), F32)
    for t in range(nsub // unroll):
        first = qi * nsub + nsub - 1 - t * unroll
        carry, acc = steps([first - u for u in range(unroll)], carry, acc, True)

    def body(t, ca):
        first = qi * nsub - 1 - t * unroll
        return steps([first - u for u in range(unroll)], ca[0], ca[1], False)

    carry, acc = lax.fori_loop(0, qi * (nsub // unroll), body, (carry, acc))
    o_ref[...] = acc * _silu(zg_ref[...])


def _sb_prompt(proj, sb_bias, bq=1024, bk=256, unroll=2):
    m = proj.shape[0]
    assert (bq // bk) % unroll == 0
    cb = lambda col: col // HD_SB
    return pl.pallas_call(
        functools.partial(_sb_prompt_kernel, bq=bq, bk=bk, unroll=unroll),
        grid=(H_SB, m // bq),
        in_specs=[pl.BlockSpec(memory_space=pltpu.SMEM),
                  pl.BlockSpec((bq, HD_SB), lambda h, i: (i, cb(COL_B_Q) + h)),
                  pl.BlockSpec((m, HD_SB), lambda h, i: (0, cb(COL_B_K) + h)),
                  pl.BlockSpec((m, HD_SB), lambda h, i: (0, cb(COL_B_V) + h)),
                  pl.BlockSpec((bq, HD_SB), lambda h, i: (i, cb(COL_B_Z) + h))],
        out_specs=pl.BlockSpec((bq, HD_SB), lambda h, i: (i, h)),
        out_shape=jax.ShapeDtypeStruct((m, BRANCH_W), F32),
        scratch_shapes=[pltpu.VMEM((m, HD_SB), BF16), pltpu.VMEM((m, HD_SB), BF16)],
        compiler_params=_params("arbitrary", "arbitrary"),
        name="sb_prompt",
    )(sb_bias, proj, proj, proj, proj)


def _sb_sample_kernel(pt_ref, bias_ref, q_ref, kn_ref, vn_ref, zg_ref, *rest, dec_seq, pages_per_step,
                      page_group):
    del pt_ref
    npg = pages_per_step
    k_refs, v_refs = rest[:npg], rest[npg:2 * npg]
    o_ref, acc_ref, carry_ref = rest[2 * npg:]
    step = pl.program_id(1)
    bk = PAGE_SIZE
    rows = H_SB * dec_seq
    jj = lax.broadcasted_iota(jnp.int32, (bk, bk), 0)
    ss = lax.broadcasted_iota(jnp.int32, (bk, bk), 1)
    suffix = jnp.where(jj > ss, 1.0, 0.0).astype(BF16)
    bias = bias_ref[...]
    head_rows = lambda x, h: x[h * dec_seq:(h + 1) * dec_seq]

    def attend(zs, get_vs, masks, carry, acc):
        weights, carry = _sb_weights(zs, carry, suffix, masks)
        for a, get_v in zip(weights, get_vs):
            outs = [jnp.dot(head_rows(a, h).astype(BF16), get_v(h).astype(BF16), preferred_element_type=F32)
                    for h in range(H_SB)]
            acc = acc + jnp.concatenate(outs, axis=0)
        return carry, acc

    @pl.when(step == 0)
    def _():
        pad = jnp.zeros((bk - dec_seq, HD_SB), F32)
        zs = []
        for h in range(H_SB):
            sl = slice(h * HD_SB, (h + 1) * HD_SB)
            kh = jnp.concatenate([kn_ref[:, sl], pad], axis=0).astype(BF16)
            qh = (q_ref[:, sl] * SB_SCALE).astype(BF16)
            zs.append(lax.dot_general(qh, kh, (((1,), (1,)), ((), ())), preferred_element_type=F32))
        new_v = lambda h: jnp.concatenate([vn_ref[:, h * HD_SB:(h + 1) * HD_SB], pad], axis=0)
        t = lax.broadcasted_iota(jnp.int32, (rows, bk), 0) % dec_seq
        pos = lax.broadcasted_iota(jnp.int32, (rows, bk), 1)
        carry, acc = attend([jnp.concatenate(zs, axis=0) + bias], [new_v], [pos < t],
                            jnp.zeros((rows, 1), F32), jnp.zeros((rows, HD_SB), F32))
        carry_ref[...] = carry
        acc_ref[...] = acc

    rid = lax.broadcasted_iota(jnp.int32, (bk, HD_SB), 0) // dec_seq
    q_rows = []
    for h in range(H_SB):
        qh = jnp.concatenate([q_ref[:, h * HD_SB:(h + 1) * HD_SB] * SB_SCALE] * (bk // dec_seq), axis=0)
        q_rows.append(jnp.where(rid == h, qh, 0.0).astype(BF16))

    def logits(pages):
        zp = None
        for h in range(H_SB):
            kh = jnp.concatenate([k_refs[p][pl.ds(h, bk, stride=H_SB), :] for p in pages], axis=0)
            part = lax.dot_general(kh.astype(BF16), q_rows[h], (((1,), (1,)), ((), ())),
                                   preferred_element_type=F32)
            zp = part if zp is None else zp + part
        return [zp[i * bk:(i + 1) * bk].T[:rows] + bias for i in range(len(pages))]

    groups = [list(range(g, g + page_group)) for g in range(0, npg, page_group)]
    carry, acc = carry_ref[...], acc_ref[...]
    zs_next = logits(groups[0])
    for gi, pages in enumerate(groups):
        zs = zs_next
        if gi + 1 < len(groups):
            zs_next = logits(groups[gi + 1])
        get_vs = [lambda h, p=p: v_refs[p][pl.ds(h, bk, stride=H_SB), :] for p in pages]
        carry, acc = attend(zs, get_vs, [None] * len(pages), carry, acc)
    carry_ref[...] = carry
    acc_ref[...] = acc

    @pl.when(step == pl.num_programs(1) - 1)
    def _():
        for h in range(H_SB):
            sl = slice(h * HD_SB, (h + 1) * HD_SB)
            o_ref[:, sl] = head_rows(acc_ref, h) * _silu(zg_ref[:, sl])


def _sb_sample(proj, cache_k, cache_v, page_table, sb_bias, layer, dec_seq, pages_per_step=16, page_group=4):
    m = proj.shape[0]
    nb = m // dec_seq
    n_pages = page_table.shape[1]
    npg = pages_per_step
    assert n_pages % npg == 0 and PAGE_SIZE % dec_seq == 0
    cb = lambda col: col // BRANCH_W
    page = lambda p: (lambda b, s, pt: (layer, pt[b, n_pages - 1 - (s * npg + p)], 0, 0))
    row = lambda col: pl.BlockSpec((dec_seq, BRANCH_W), lambda b, s, pt, col=col: (b, cb(col)))
    rows = H_SB * dec_seq
    bias_tile = jnp.broadcast_to(jnp.repeat(sb_bias, dec_seq)[:, None], (rows, PAGE_SIZE))
    page_spec = lambda p: pl.BlockSpec((None, None, PAGE_SIZE * H_SB, HD_SB), page(p))
    grid_spec = pltpu.PrefetchScalarGridSpec(
        num_scalar_prefetch=1,
        grid=(nb, n_pages // npg),
        in_specs=[pl.BlockSpec((rows, PAGE_SIZE), lambda b, s, pt: (0, 0)),
                  row(COL_B_Q), row(COL_B_K), row(COL_B_V), row(COL_B_Z)]
                 + [page_spec(p) for p in range(npg)] + [page_spec(p) for p in range(npg)],
        out_specs=pl.BlockSpec((dec_seq, BRANCH_W), lambda b, s, pt: (b, 0)),
        scratch_shapes=[pltpu.VMEM((rows, HD_SB), F32), pltpu.VMEM((rows, 1), F32)],
    )
    return pl.pallas_call(
        functools.partial(_sb_sample_kernel, dec_seq=dec_seq, pages_per_step=npg, page_group=page_group),
        grid_spec=grid_spec,
        out_shape=jax.ShapeDtypeStruct((m, BRANCH_W), F32),
        compiler_params=_params("arbitrary", "arbitrary"),
        name="sb_sample",
    )(page_table, bias_tile, proj, proj, proj, proj, *([cache_k] * npg), *([cache_v] * npg))


def _softplus(x):
    return jnp.maximum(x, 0.0) + jnp.log(1.0 + jnp.exp(-jnp.abs(x)))


def _dot_exact_rhs(a_f32, sel_bf16):
    hi, mid, lo = _split3(a_f32)
    d = lambda p: jnp.dot(p, sel_bf16, preferred_element_type=F32)
    return d(hi) + (d(mid) + d(lo))


def _dot_exact_lhs(sel_bf16, b_f32):
    hi, mid, lo = _split3(b_f32)
    d = lambda p: jnp.dot(sel_bf16, p, preferred_element_type=F32)
    return d(hi) + (d(mid) + d(lo))


def _dot_x3(a, b):
    ah, al = _split2(a)
    bh, bl = _split2(b)
    d = lambda x, y: jnp.dot(x, y, preferred_element_type=F32)
    return d(ah, bh) + (d(ah, bl) + d(al, bh))


def _dn_prep_tail(seg_inputs, w_ref, ps_ref, alog_ref, dtb_ref, eg_ref, eb_ref,
                  qn_ref, kn_ref, vc_ref, g_ref, beta_ref):
    outs = (qn_ref, kn_ref, vc_ref)
    for seg, (x0, x1, x2, x3) in enumerate(seg_inputs):
        sl = slice(seg * BRANCH_W, (seg + 1) * BRANCH_W)
        w = w_ref[:, sl]
        y = _silu(((x3 * w[0:1] + x2 * w[1:2]) + x1 * w[2:3]) + x0 * w[3:4])
        if seg < 2:
            scale = DK_DN ** -0.5 if seg == 0 else 1.0
            for h in range(H_DN):
                hs = slice(h * DK_DN, (h + 1) * DK_DN)
                yh = y[:, hs]
                ss = jnp.sum(yh * yh, axis=-1, keepdims=True)
                outs[seg][:, hs] = yh * lax.rsqrt(ss + RMS_EPS) * scale
        else:
            outs[seg][...] = y
    ps = ps_ref[...]
    g = -jnp.exp(alog_ref[...]) * _softplus(ps + dtb_ref[...])
    g_ref[...] = _dot_exact_rhs(g, eg_ref[...])
    beta_ref[...] = _dot_exact_rhs(_sigmoid(ps), eb_ref[...])


def _dn_prep_prompt_kernel(xq_ref, xk_ref, xv_ref, hq_ref, hk_ref, hv_ref, *rest):
    first = pl.program_id(0) == 0
    segs = []
    for x_ref, h_ref in ((xq_ref, hq_ref), (xk_ref, hk_ref), (xv_ref, hv_ref)):
        x = x_ref[...]
        halo = jnp.where(first, 0.0, h_ref[...])
        rid = lax.broadcasted_iota(jnp.int32, halo.shape, 0)
        shifted = []
        for k in (1, 2, 3):
            xr = pltpu.roll(x, k, 0)
            head = jnp.where(rid < k, pltpu.roll(halo, k, 0), xr[:8])
            shifted.append(jnp.concatenate([head, xr[8:]], axis=0))
        segs.append((x, *shifted))
    _dn_prep_tail(segs, *rest)


def _dn_prep_sample_kernel(xq_ref, xk_ref, xv_ref, *rest):
    shifted = rest[:9]
    segs = []
    for s, x_ref in enumerate((xq_ref, xk_ref, xv_ref)):
        segs.append((x_ref[...], shifted[s][...], shifted[3 + s][...], shifted[6 + s][...]))
    _dn_prep_tail(segs, *rest[9:])


def _dn_prep(proj, proj_small, shifted, conv_w, a_log, dt_bias, tm=256):
    m = proj.shape[0]
    cb = lambda col: col // BRANCH_W
    xblk = lambda s: pl.BlockSpec((tm, BRANCH_W), lambda i, s=s: (i, cb(COL_C_QKV) + s))
    in_specs = [xblk(0), xblk(1), xblk(2)]
    args = [proj, proj, proj]
    if shifted is None:
        hblk = lambda s: pl.BlockSpec(
            (8, BRANCH_W), lambda i, s=s: (jnp.maximum(i * (tm // 8) - 1, 0), cb(COL_C_QKV) + s))
        in_specs += [hblk(0), hblk(1), hblk(2)]
        args += [proj, proj, proj]
        body = _dn_prep_prompt_kernel
    else:
        for arr in shifted:
            for s in range(3):
                in_specs.append(pl.BlockSpec((tm, BRANCH_W), lambda i, s=s: (i, s)))
                args.append(arr)
        body = _dn_prep_sample_kernel
    lane = lax.broadcasted_iota(jnp.int32, (LANES, BRANCH_W), 0)
    head = lax.broadcasted_iota(jnp.int32, (LANES, BRANCH_W), 1) // DV_DN
    e_g = (lane == head).astype(BF16)
    e_b = (lane == head + H_DN).astype(BF16)
    pad = lambda v, off: jnp.zeros((1, LANES), F32).at[0, off:off + H_DN].set(v)
    const = lambda shape: pl.BlockSpec(shape, lambda i: (0, 0))
    in_specs += [const((CONV_W, D_DN_CONV)), pl.BlockSpec((tm, LANES), lambda i: (i, 0)),
                 const((1, LANES)), const((1, LANES)), const((LANES, BRANCH_W)), const((LANES, BRANCH_W))]
    args += [conv_w, proj_small, pad(a_log, 0), pad(dt_bias, 0), e_g, e_b]
    out = pl.BlockSpec((tm, BRANCH_W), lambda i: (i, 0))
    return pl.pallas_call(
        body,
        grid=(m // tm,),
        in_specs=in_specs,
        out_specs=[out] * 5,
        out_shape=[jax.ShapeDtypeStruct((m, BRANCH_W), F32)] * 5,
        compiler_params=_params("arbitrary"),
        name="dn_prep",
    )(*args)


def _dn_masks(c, block):
    i = lax.broadcasted_iota(jnp.int32, (c, c), 0)
    j = lax.broadcasted_iota(jnp.int32, (c, c), 1)
    same = (i // block) == (j // block)
    return dict(incl=same & (i >= j), strict=same & (i > j), incl_t=same & (i <= j), same=same,
                eye=(i == j))


def _dn_chunk_core(qs, ks, vs, g_reps, beta_reps, masks, n_double):
    c = qs[0].shape[0]
    each = lambda f, *lists: [f(*args) for args in zip(*lists)]
    onef = lambda m: jnp.where(m, 1.0, 0.0).astype(BF16)
    incl = masks["incl"]
    incl_b, same_b, ones_b = onef(incl), onef(masks["same"]), jnp.ones((c, c), BF16)
    contract_last = lambda x, y: lax.dot_general(x, y, (((1,), (1,)), ((), ())), preferred_element_type=F32)
    gcums = each(lambda g: _dot_exact_lhs(incl_b, g), g_reps)
    glasts = each(lambda g: _dot_exact_lhs(same_b, g), g_reps)
    g_rows = each(lambda g: _dot_exact_lhs(ones_b, jnp.where(masks["incl_t"], g[:, :c], 0.0)), g_reps)
    decays = each(lambda gc, gr: jnp.where(incl, jnp.exp(jnp.where(incl, gc[:, :c] - gr, 0.0)), 0.0),
                  gcums, g_rows)
    kbs = each(lambda k: k.astype(BF16), ks)
    kks = each(lambda kb: contract_last(kb, kb), kbs)
    qks = each(lambda q, kb, d: contract_last(q.astype(BF16), kb) * d, qs, kbs, decays)
    eye = jnp.where(masks["eye"], 1.0, 0.0)
    ps = each(lambda b, kk, d: jnp.where(masks["strict"], -(b[:, :c] * kk * d), 0.0), beta_reps, kks, decays)
    xs = each(lambda p: eye + p, ps)
    for _ in range(n_double):
        ps = each(lambda p: _dot_x3(p, p), ps)
        xs = each(lambda x, p: x + _dot_x3(x, p), xs, ps)
    gams = each(jnp.exp, gcums)
    u_parts = each(lambda x, b, v: _dot_x3(x, b * v), xs, beta_reps, vs)
    w_parts = each(lambda x, b, gam, k: _dot_x3(x, (b * gam) * k), xs, beta_reps, gams, ks)
    q_gs = each(lambda q, gam: q * gam, qs, gams)
    k_ends = each(lambda k, gl, gc: k * jnp.exp(gl - gc), ks, glasts, gcums)
    g_ends = each(jnp.exp, glasts)
    return u_parts, w_parts, qks, q_gs, k_ends, g_ends


def _gated_rms(o, norm_g, z):
    ms = jnp.mean(o * o, axis=-1, keepdims=True)
    return o * lax.rsqrt(ms + RMS_EPS) * norm_g * _silu(z)


def _dn_prompt_kernel(q_ref, k_ref, v_ref, g_ref, b_ref, z_ref, ng_ref, o_ref, sfin_ref, s_ref):
    n = pl.program_id(0)

    @pl.when(n == 0)
    def _():
        s_ref[...] = jnp.zeros_like(s_ref)

    c = q_ref.shape[0]
    masks = _dn_masks(c, c)
    heads = range(H_DN)
    cols = [slice(h * DV_DN, (h + 1) * DV_DN) for h in heads]
    take = lambda ref: [ref[:, hs] for hs in cols]
    u_ps, w_ps, qks, q_gs, k_ends, g_ends = _dn_chunk_core(
        take(q_ref), take(k_ref), take(v_ref), take(g_ref), take(b_ref), masks, int(math.log2(c)) - 1)
    dot = lambda x, y: jnp.dot(x, y, preferred_element_type=F32)
    ss = [s_ref[h] for h in heads]
    sbs = [s.astype(BF16) for s in ss]
    ubs = [(u_ps[h] - dot(w_ps[h].astype(BF16), sbs[h])).astype(BF16) for h in heads]
    os_ = [dot(q_gs[h].astype(BF16), sbs[h]) + dot(qks[h].astype(BF16), ubs[h]) for h in heads]
    for h in heads:
        s_ref[h] = g_ends[h][0:1, :] * ss[h] + lax.dot_general(
            k_ends[h].astype(BF16), ubs[h], (((0,), (0,)), ((), ())), preferred_element_type=F32)
        o_ref[:, cols[h]] = _gated_rms(os_[h], ng_ref[...], z_ref[:, cols[h]])

    @pl.when(n == pl.num_programs(0) - 1)
    def _():
        sfin_ref[...] = s_ref[...]


def _dn_prompt(qn, kn, vc, g_rep, beta_rep, proj, norm_g):
    m = qn.shape[0]
    c = DN_CHUNK
    blk = pl.BlockSpec((c, BRANCH_W), lambda n: (n, 0))
    zblk = pl.BlockSpec((c, BRANCH_W), lambda n: (n, COL_C_Z // BRANCH_W))
    state = pl.BlockSpec((H_DN, DK_DN, DV_DN), lambda n: (0, 0, 0))
    return pl.pallas_call(
        _dn_prompt_kernel,
        grid=(m // c,),
        in_specs=[blk, blk, blk, blk, blk, zblk, pl.BlockSpec((1, DV_DN), lambda n: (0, 0))],
        out_specs=[blk, state],
        out_shape=[jax.ShapeDtypeStruct((m, BRANCH_W), F32),
                   jax.ShapeDtypeStruct((H_DN, DK_DN, DV_DN), F32)],
        scratch_shapes=[pltpu.VMEM((H_DN, DK_DN, DV_DN), F32)],
        compiler_params=_params("arbitrary"),
        name="dn_prompt",
    )(qn, kn, vc, g_rep, beta_rep, proj, norm_g.reshape(1, DV_DN))


def _dn_sample_kernel(q_ref, k_ref, v_ref, g_ref, b_ref, z_ref, ng_ref, s0_ref, o_ref, snew_ref, *, dec_seq, n_seq):
    t = dec_seq
    rows = H_DN * t
    seqs, heads = range(n_seq), range(H_DN)
    cols = [slice(h * DV_DN, (h + 1) * DV_DN) for h in heads]
    srows = [slice(i * t, (i + 1) * t) for i in seqs]
    take = lambda ref: [jnp.concatenate([ref[srows[i], hs] for hs in cols], axis=0) for i in seqs]
    head_rows = lambda x, h: x[h * t:(h + 1) * t]
    masks = _dn_masks(rows, t)
    u_ps, w_ps, qks, q_gs, k_ends, g_ends = _dn_chunk_core(
        take(q_ref), take(k_ref), take(v_ref), take(g_ref), take(b_ref), masks, max(int(math.log2(t)) - 1, 0))
    dot = lambda x, y: jnp.dot(x, y, preferred_element_type=F32)
    sbs = [[s0_ref[i, h].astype(BF16) for h in heads] for i in seqs]
    ubs = [jnp.concatenate([head_rows(u_ps[i], h) - dot(head_rows(w_ps[i], h).astype(BF16), sbs[i][h])
                            for h in heads], axis=0).astype(BF16) for i in seqs]
    os_ = [jnp.concatenate([dot(head_rows(q_gs[i], h).astype(BF16), sbs[i][h]) for h in heads], axis=0)
           + dot(qks[i].astype(BF16), ubs[i]) for i in seqs]
    rid = lax.broadcasted_iota(jnp.int32, (rows, DK_DN), 0) // t
    for i in seqs:
        for h in heads:
            kh = jnp.where(rid == h, k_ends[i], 0.0).astype(BF16)
            upd = lax.dot_general(kh, ubs[i], (((0,), (0,)), ((), ())), preferred_element_type=F32)
            snew_ref[i, h] = head_rows(g_ends[i], h)[0:1, :] * s0_ref[i, h] + upd
            o_ref[srows[i], cols[h]] = _gated_rms(head_rows(os_[i], h), ng_ref[...], z_ref[srows[i], cols[h]])


def _dn_sample(qn, kn, vc, g_rep, beta_rep, proj, norm_g, s0, layer, dec_seq, n_seq=4):
    m = qn.shape[0]
    nb = m // dec_seq
    assert nb % n_seq == 0
    blk = pl.BlockSpec((n_seq * dec_seq, BRANCH_W), lambda b: (b, 0))
    zblk = pl.BlockSpec((n_seq * dec_seq, BRANCH_W), lambda b: (b, COL_C_Z // BRANCH_W))
    return pl.pallas_call(
        functools.partial(_dn_sample_kernel, dec_seq=dec_seq, n_seq=n_seq),
        grid=(nb // n_seq,),
        in_specs=[blk, blk, blk, blk, blk, zblk, pl.BlockSpec((1, DV_DN), lambda b: (0, 0)),
                  pl.BlockSpec((None, n_seq, H_DN, DK_DN, DV_DN), lambda b: (layer, b, 0, 0, 0))],
        out_specs=[blk, pl.BlockSpec((n_seq, H_DN, DK_DN, DV_DN), lambda b: (b, 0, 0, 0))],
        out_shape=[jax.ShapeDtypeStruct((m, BRANCH_W), F32),
                   jax.ShapeDtypeStruct((nb, H_DN, DK_DN, DV_DN), F32)],
        compiler_params=_params("arbitrary"),
        name="dn_sample",
    )(qn, kn, vc, g_rep, beta_rep, proj, norm_g.reshape(1, DV_DN), s0)


def _mixer_layer(x, p, sample, tm):
    m = x.shape[0]
    proj = _matmul(x, p["w_main"], tm, 1024, "proj_main")
    proj_small = _matmul(x, p["w_small"], tm, LANES, "proj_small")

    if sample is None:
        w_mix, bias_full = p["sg_w"], p["sg_bias_prompt"]
    else:
        w_mix, bias_full = p["sg_w_sample"], p["sg_bias_sample"]
    o_a, sg_v = _spatial_gating(proj, w_mix, bias_full, p["sg_ln_g"], p["sg_ln_b"])

    conv_in = proj[:, COL_C_QKV:COL_C_QKV + D_DN_CONV]
    if sample is None:
        o_b = _sb_prompt(proj, p["sb_bias"])
        qn, kn, vc, g_rep, beta_rep = _dn_prep(proj, proj_small, None, p["dn_conv_w"], p["dn_a_log"],
                                               p["dn_dt_bias"])
        o_c, s_new = _dn_prompt(qn, kn, vc, g_rep, beta_rep, proj, p["dn_norm_g"])
        conv_new = conv_in[m - (CONV_W - 1):][None]
        s_new = s_new[None]
    else:
        t = sample["dec_seq"]
        nb = m // t
        o_b = _sb_sample(proj, sample["cache_k"], sample["cache_v"], sample["page_table"], p["sb_bias"],
                         sample["layer"], t)
        xp = jnp.concatenate([sample["conv"], conv_in.reshape(nb, t, D_DN_CONV)], axis=1)
        shifted = [xp[:, CONV_W - 1 - k:CONV_W - 1 - k + t].reshape(m, D_DN_CONV) for k in (1, 2, 3)]
        qn, kn, vc, g_rep, beta_rep = _dn_prep(proj, proj_small, shifted, p["dn_conv_w"], p["dn_a_log"],
                                               p["dn_dt_bias"])
        o_c, s_new = _dn_sample(qn, kn, vc, g_rep, beta_rep, proj, p["dn_norm_g"], sample["ssm"],
                                sample["layer"], t)
        conv_new = xp[:, t:]

    y = _merge(o_a, o_b, o_c, proj, p["w_branch"], p["gate_b"])
    x_new = _out_proj(y, p["w_out"], x, p["ln_g"], p["ln_b"])
    k_rows = proj[:, COL_B_K:COL_B_K + BRANCH_W]
    v_rows = proj[:, COL_B_V:COL_B_V + BRANCH_W]
    return x_new, k_rows, v_rows, conv_new, s_new, sg_v


def kernel(x_prompt, x_sample, cache_sb_k, cache_sb_v, state_dn_conv, state_dn_ssm, page_table, ln_in_g, ln_in_b,
           w_in, sg_ln_g, sg_ln_b, sg_w, sg_b, sb_bias, dn_conv_w, dn_a_log, dn_dt_bias, dn_norm_g, gate_b,
           w_branch, w_out, ln_g, ln_b):
    bp, seq, d = x_prompt.shape
    bs, dec_seq, _ = x_sample.shape
    assert bp == 1, "prompt rows are treated as one sequence"
    depth = w_in.shape[0]
    n_pool = cache_sb_k.shape[1]

    xp = _layer_norm(x_prompt.reshape(bp * seq, d), ln_in_g, ln_in_b)
    xs = _layer_norm(x_sample.reshape(bs * dec_seq, d), ln_in_g, ln_in_b)
    cache_k = cache_sb_k.reshape(depth, n_pool, PAGE_SIZE * H_SB, HD_SB)
    cache_v = cache_sb_v.reshape(depth, n_pool, PAGE_SIZE * H_SB, HD_SB)

    reps = SG_CHUNK // dec_seq
    eye = jnp.eye(reps, dtype=F32)
    outs = [[] for _ in range(9)]
    for l in range(depth):
        small = w_in[l][:, COL_SMALL:COL_SMALL + 2 * H_DN]
        sg_small = jnp.tril(sg_w[l][:, :dec_seq, :dec_seq])
        p = {
            "w_main": jnp.concatenate([w_in[l][:, :COL_SMALL], w_in[l][:, COL_SMALL + 2 * H_DN:]],
                                      axis=1).astype(BF16),
            "w_small": jnp.pad(small, ((0, 0), (0, LANES - 2 * H_DN))).astype(BF16),
            "sg_w": sg_w[l],
            "sg_bias_prompt": jnp.repeat(sg_b[l].T, BRANCH_W // SG_GROUPS, axis=1),
            "sg_w_sample": jax.vmap(lambda w: jnp.kron(eye, w))(sg_small),
            "sg_bias_sample": jnp.repeat(jnp.tile(sg_b[l][:, :dec_seq].T, (reps, 1)),
                                         BRANCH_W // SG_GROUPS, axis=1),
            "sg_ln_g": sg_ln_g[l], "sg_ln_b": sg_ln_b[l], "sb_bias": sb_bias[l],
            "dn_conv_w": dn_conv_w[l], "dn_a_log": dn_a_log[l], "dn_dt_bias": dn_dt_bias[l],
            "dn_norm_g": dn_norm_g[l], "gate_b": gate_b[l],
            "w_branch": w_branch[l].astype(BF16), "w_out": w_out[l].astype(BF16),
            "ln_g": ln_g[l], "ln_b": ln_b[l],
        }
        xp, kp, vp, cp, sp, _ = _mixer_layer(xp, p, None, tm=1024)
        sample = {"cache_k": cache_k, "cache_v": cache_v, "page_table": page_table,
                  "conv": state_dn_conv[l], "ssm": state_dn_ssm, "layer": l, "dec_seq": dec_seq}
        xs, ks, vs, cs, ss, sgv = _mixer_layer(xs, p, sample, tm=1024)
        hd = (H_SB, HD_SB)
        for lst, val in zip(outs, (kp.reshape(bp, seq, *hd), vp.reshape(bp, seq, *hd),
                                   ks.reshape(bs, dec_seq, *hd), vs.reshape(bs, dec_seq, *hd),
                                   cp, cs, sp, ss, sgv.reshape(bs, dec_seq, BRANCH_W))):
            lst.append(val)
    return (xp.reshape(bp, seq, d), xs.reshape(bs, dec_seq, d), *[jnp.stack(o) for o in outs])
```

```python
import functools
import math

import jax
import jax.numpy as jnp
from jax import lax
from jax.experimental import pallas as pl
from jax.experimental.pallas import tpu as pltpu

F32 = jnp.float32
BF16 = jnp.bfloat16

VMEM_LIMIT_BYTES = 56 * 1024 * 1024
LANES = 128

D_MODEL = 2048
BRANCH_W = 1024
N_BRANCH = 3
SG_GROUPS = 8
SG_CHUNK = 128
H_SB = 8
HD_SB = 128
H_DN = 8
DK_DN = 128
DV_DN = 128
D_DN_CONV = 3 * BRANCH_W
CONV_W = 4
DN_CHUNK = 64
PAGE_SIZE = 128
LN_EPS = 1e-5
RMS_EPS = 1e-6
DEPTH = 2
ALPHA = (2 * DEPTH) ** 0.25

COL_A_U, COL_A_V, COL_A_Z = 0, 1024, 2048
COL_B_Q, COL_B_K, COL_B_V, COL_B_Z = 3072, 4096, 5120, 6144
COL_C_QKV, COL_C_Z = 7168, 10240
N_MAIN = 11264
COL_SMALL = 11264
COL_GATE = COL_SMALL + 2 * H_DN
LOG2E = 1.4426950408889634


def _params(*sem):
    return pltpu.CompilerParams(dimension_semantics=sem, vmem_limit_bytes=VMEM_LIMIT_BYTES)


def _silu(x):
    return x * (1.0 / (1.0 + jnp.exp(-x)))


def _sigmoid(x):
    return 1.0 / (1.0 + jnp.exp(-x))


def _gelu(x):
    c = math.sqrt(2.0 / math.pi)
    return 0.5 * x * (1.0 + jnp.tanh(c * (x + 0.044715 * (x * x * x))))


def _layer_norm_rows(x, g, b):
    mu = jnp.mean(x, axis=-1, keepdims=True)
    xc = x - mu
    var = jnp.mean(xc * xc, axis=-1, keepdims=True)
    return xc * lax.rsqrt(var + LN_EPS) * g + b


def _ln_kernel(x_ref, g_ref, b_ref, o_ref):
    o_ref[...] = _layer_norm_rows(x_ref[...], g_ref[...], b_ref[...])


def _layer_norm(x, g, b, tm=512):
    m, d = x.shape
    tm = min(tm, m)
    return pl.pallas_call(
        _ln_kernel,
        grid=(m // tm,),
        in_specs=[pl.BlockSpec((tm, d), lambda i: (i, 0)),
                  pl.BlockSpec((1, d), lambda i: (0, 0)),
                  pl.BlockSpec((1, d), lambda i: (0, 0))],
        out_specs=pl.BlockSpec((tm, d), lambda i: (i, 0)),
        out_shape=jax.ShapeDtypeStruct((m, d), F32),
        compiler_params=_params("arbitrary"),
        name="ln_in",
    )(x, g.reshape(1, d), b.reshape(1, d))


def _mm_kernel(x_ref, w_ref, o_ref, xb_ref):
    @pl.when(pl.program_id(1) == 0)
    def _():
        xb_ref[...] = x_ref[...].astype(BF16)

    o_ref[...] = jnp.dot(xb_ref[...], w_ref[...].astype(BF16), preferred_element_type=F32)


def _matmul(x, w, layer, n, tm, tn, name):
    m, k = x.shape
    tm = min(tm, m)
    return pl.pallas_call(
        _mm_kernel,
        grid=(m // tm, n // tn),
        in_specs=[pl.BlockSpec((tm, k), lambda i, j: (i, 0)),
                  pl.BlockSpec((None, k, tn), lambda i, j: (layer, 0, j))],
        out_specs=pl.BlockSpec((tm, tn), lambda i, j: (i, j)),
        out_shape=jax.ShapeDtypeStruct((m, n), F32),
        scratch_shapes=[pltpu.VMEM((tm, k), BF16)],
        compiler_params=_params("arbitrary", "arbitrary"),
        name=name,
    )(x, w)


def _sg_kernel(u_ref, v_ref, z_ref, w_ref, bias_ref, g_ref, b_ref, o_ref, vn_ref):
    u = _gelu(u_ref[...])
    vn = _layer_norm_rows(_gelu(v_ref[...]), g_ref[...], b_ref[...])
    vn_ref[...] = vn
    vb = vn.astype(BF16)
    rows = vn.shape[0]
    r = lax.broadcasted_iota(jnp.int32, (rows, rows), 0)
    c = lax.broadcasted_iota(jnp.int32, (rows, rows), 1)
    lower = c <= r
    parts = []
    for grp in range(SG_GROUPS):
        w = jnp.where(lower, w_ref[grp], 0.0).astype(BF16)
        parts.append(jnp.dot(w, vb[:, grp * LANES:(grp + 1) * LANES], preferred_element_type=F32))
    mixed = jnp.concatenate(parts, axis=-1) + bias_ref[...]
    o_ref[...] = (u * mixed * _silu(z_ref[...])).astype(o_ref.dtype)


def _spatial_gating(proj, w_mix, bias_full, ln_g, ln_b):
    m = proj.shape[0]
    tm = SG_CHUNK
    blk = lambda col: pl.BlockSpec((tm, BRANCH_W), lambda i, col=col: (i, col // BRANCH_W))
    return pl.pallas_call(
        _sg_kernel,
        grid=(m // tm,),
        in_specs=[blk(COL_A_U), blk(COL_A_V), blk(COL_A_Z),
                  pl.BlockSpec((SG_GROUPS, tm, tm), lambda i: (0, 0, 0)),
                  pl.BlockSpec((tm, BRANCH_W), lambda i: (0, 0)),
                  pl.BlockSpec((1, BRANCH_W), lambda i: (0, 0)),
                  pl.BlockSpec((1, BRANCH_W), lambda i: (0, 0))],
        out_specs=[pl.BlockSpec((tm, BRANCH_W), lambda i: (i, 0)),
                   pl.BlockSpec((tm, BRANCH_W), lambda i: (i, 0))],
        out_shape=[jax.ShapeDtypeStruct((m, BRANCH_W), BF16),
                   jax.ShapeDtypeStruct((m, BRANCH_W), F32)],
        compiler_params=_params("arbitrary"),
        name="spatial_gating",
    )(proj, proj, proj, w_mix, bias_full, ln_g.reshape(1, -1), ln_b.reshape(1, -1))


def _merge_out_kernel(oa_ref, ob_ref, oc_ref, wb_ref, ga_ref, gb_ref, gc_ref, gbias_ref, wo_ref, x_ref,
                      g_ref, b_ref, o_ref):
    merged = None
    for n, (o_n, gate_ref) in enumerate(((oa_ref, ga_ref), (ob_ref, gb_ref), (oc_ref, gc_ref))):
        y = jnp.dot(o_n[...].astype(BF16), wb_ref[n], preferred_element_type=F32)
        t = _sigmoid(gate_ref[...] + gbias_ref[n:n + 1, :]) * y
        merged = t if merged is None else merged + t
    out = jnp.dot(merged.astype(BF16), wo_ref[...], preferred_element_type=F32)
    o_ref[...] = _layer_norm_rows(ALPHA * x_ref[...] + out, g_ref[...], b_ref[...])


def _merge_out(o_a, o_b, o_c, gates, w_branch, gate_b, w_out, x, ln_g, ln_b, tm=256):
    m, d = x.shape
    tm = min(tm, m)
    resident = lambda shape: pl.BlockSpec(shape, lambda i: (0,) * len(shape), pipeline_mode=pl.Buffered(1))
    oblk = pl.BlockSpec((tm, BRANCH_W), lambda i: (i, 0))
    gblk = lambda n: pl.BlockSpec((tm, d), lambda i, n=n: (i, n))
    row = pl.BlockSpec((tm, d), lambda i: (i, 0))
    vec = pl.BlockSpec((1, d), lambda i: (0, 0))
    return pl.pallas_call(
        _merge_out_kernel,
        grid=(m // tm,),
        in_specs=[oblk, oblk, oblk, resident((N_BRANCH, BRANCH_W, d)), gblk(0), gblk(1), gblk(2),
                  pl.BlockSpec((N_BRANCH, d), lambda i: (0, 0)), resident((d, d)), row, vec, vec],
        out_specs=row,
        out_shape=jax.ShapeDtypeStruct((m, d), F32),
        compiler_params=_params("arbitrary"),
        name="merge_out",
    )(o_a, o_b, o_c, w_branch, gates, gates, gates, gate_b, w_out, x, ln_g.reshape(1, d), ln_b.reshape(1, d))


SB_SCALE = HD_SB ** -0.5 * LOG2E


def _split2(x):
    hi = x.astype(BF16)
    lo = (x - hi.astype(F32)).astype(BF16)
    return hi, lo


def _split3(x):
    hi = x.astype(BF16)
    r = x - hi.astype(F32)
    mid = r.astype(BF16)
    lo = (r - mid.astype(F32)).astype(BF16)
    return hi, mid, lo


def _suffix_matrix(bk):
    j = lax.broadcasted_iota(jnp.int32, (2 * bk, bk), 0) % bk
    s = lax.broadcasted_iota(jnp.int32, (2 * bk, bk), 1)
    return jnp.where(j > s, 1.0, 0.0).astype(BF16)


def _sb_weights(zs, spent, suffix, masks):
    sign = jnp.uint32(0x80000000)
    log_betas, costs, pieces = [], [], []
    for z, mask in zip(zs, masks):
        neg_abs = lax.bitcast_convert_type(lax.bitcast_convert_type(z, jnp.uint32) | sign, F32)
        cost = jnp.maximum(z, 0.0) + jnp.log2(1.0 + jnp.exp2(neg_abs))
        log_betas.append(z - cost)
        if mask is not None:
            cost = jnp.where(mask, cost, 0.0)
        costs.append(cost)
        pieces.append(jnp.concatenate(_split2(cost), axis=1))
    withins = [jnp.dot(piece, suffix, preferred_element_type=F32) for piece in pieces]
    weights = []
    for log_beta, cost, within, mask in zip(log_betas, costs, withins, masks):
        a = jnp.exp2(log_beta - (spent + within))
        weights.append(a if mask is None else jnp.where(mask, a, 0.0))
        spent = spent + jnp.sum(cost, axis=-1, keepdims=True)
    return weights, spent


def _sb_prompt_kernel(bias_ref, q_ref, k_ref, v_ref, zg_ref, o_ref, ko_ref, vo_ref, kb_ref, vb_ref, *, bq, bk,
                      unroll):
    h = pl.program_id(0)
    qi = pl.program_id(1)
    nsub = bq // bk

    @pl.when(qi == 0)
    def _():
        k = k_ref[...]
        v = v_ref[...]
        ko_ref[...] = k
        vo_ref[...] = v
        kb_ref[...] = k.astype(BF16)
        vb_ref[...] = v.astype(BF16)

    bias = bias_ref[h] * LOG2E
    qb = (q_ref[...] * SB_SCALE).astype(BF16)
    suffix = _suffix_matrix(bk)
    q_pos = qi * bq + lax.broadcasted_iota(jnp.int32, (bq, bk), 0)
    k_off = lax.broadcasted_iota(jnp.int32, (bq, bk), 1)

    def steps(js, carry, acc, masked):
        starts = [pl.multiple_of(j * bk, bk) for j in js]
        zs = [lax.dot_general(qb, kb_ref[pl.ds(st, bk), :], (((1,), (1,)), ((), ())),
                              preferred_element_type=F32) + bias for st in starts]
        masks = [(j * bk + k_off) < q_pos if masked else None for j in js]
        weights, carry = _sb_weights(zs, carry, suffix, masks)
        for a, st in zip(weights, starts):
            acc = acc + jnp.dot(a.astype(BF16), vb_ref[pl.ds(st, bk), :], preferred_element_type=F32)
        return carry, acc

    carry = jnp.zeros((bq, 1), F32)
    acc = jnp.zeros((bq, HD_SB), F32)
    for t in range(nsub // unroll):
        first = qi * nsub + nsub - 1 - t * unroll
        carry, acc = steps([first - u for u in range(unroll)], carry, acc, True)

    def body(t, ca):
        first = qi * nsub - 1 - t * unroll
        return steps([first - u for u in range(unroll)], ca[0], ca[1], False)

    carry, acc = lax.fori_loop(0, qi * (nsub // unroll), body, (carry, acc))
    o_ref[...] = (acc * _silu(zg_ref[...])).astype(o_ref.dtype)


def _sb_prompt(proj, sb_bias, bq=1024, bk=256, unroll=2):
    m = proj.shape[0]
    assert (bq // bk) % unroll == 0
    cb = lambda col: col // HD_SB
    return pl.pallas_call(
        functools.partial(_sb_prompt_kernel, bq=bq, bk=bk, unroll=unroll),
        grid=(H_SB, m // bq),
        in_specs=[pl.BlockSpec(memory_space=pltpu.SMEM),
                  pl.BlockSpec((bq, HD_SB), lambda h, i: (i, cb(COL_B_Q) + h)),
                  pl.BlockSpec((m, HD_SB), lambda h, i: (0, cb(COL_B_K) + h)),
                  pl.BlockSpec((m, HD_SB), lambda h, i: (0, cb(COL_B_V) + h)),
                  pl.BlockSpec((bq, HD_SB), lambda h, i: (i, cb(COL_B_Z) + h))],
        out_specs=[pl.BlockSpec((bq, HD_SB), lambda h, i: (i, h)),
                   pl.BlockSpec((m, HD_SB), lambda h, i: (0, h)),
                   pl.BlockSpec((m, HD_SB), lambda h, i: (0, h))],
        out_shape=[jax.ShapeDtypeStruct((m, BRANCH_W), BF16),
                   jax.ShapeDtypeStruct((m, BRANCH_W), F32),
                   jax.ShapeDtypeStruct((m, BRANCH_W), F32)],
        scratch_shapes=[pltpu.VMEM((m, HD_SB), BF16), pltpu.VMEM((m, HD_SB), BF16)],
        compiler_params=_params("arbitrary", "arbitrary"),
        name="sb_prompt",
    )(sb_bias, proj, proj, proj, proj)


def _sb_sample_kernel(pt_ref, bias_ref, q_ref, kn_ref, vn_ref, zg_ref, *rest, dec_seq, pages_per_step,
                      page_group):
    del pt_ref
    npg = pages_per_step
    k_refs, v_refs = rest[:npg], rest[npg:2 * npg]
    o_ref, acc_ref, carry_ref = rest[2 * npg:]
    step = pl.program_id(1)
    bk = PAGE_SIZE
    rows = H_SB * dec_seq
    suffix = _suffix_matrix(bk)
    bias = bias_ref[...]
    head_rows = lambda x, h: x[h * dec_seq:(h + 1) * dec_seq]

    def attend(zs, get_vs, masks, carry, acc):
        weights, carry = _sb_weights(zs, carry, suffix, masks)
        for a, get_v in zip(weights, get_vs):
            outs = [jnp.dot(head_rows(a, h).astype(BF16), get_v(h).astype(BF16), preferred_element_type=F32)
                    for h in range(H_SB)]
            acc = acc + jnp.concatenate(outs, axis=0)
        return carry, acc

    @pl.when(step == 0)
    def _():
        pad = jnp.zeros((bk - dec_seq, HD_SB), F32)
        zs = []
        for h in range(H_SB):
            sl = slice(h * HD_SB, (h + 1) * HD_SB)
            kh = jnp.concatenate([kn_ref[:, sl], pad], axis=0).astype(BF16)
            qh = (q_ref[:, sl] * SB_SCALE).astype(BF16)
            zs.append(lax.dot_general(qh, kh, (((1,), (1,)), ((), ())), preferred_element_type=F32))
        new_v = lambda h: jnp.concatenate([vn_ref[:, h * HD_SB:(h + 1) * HD_SB], pad], axis=0)
        t = lax.broadcasted_iota(jnp.int32, (rows, bk), 0) % dec_seq
        pos = lax.broadcasted_iota(jnp.int32, (rows, bk), 1)
        carry, acc = attend([jnp.concatenate(zs, axis=0) + bias], [new_v], [pos < t],
                            jnp.zeros((rows, 1), F32), jnp.zeros((rows, HD_SB), F32))
        carry_ref[...] = carry
        acc_ref[...] = acc

    rid = lax.broadcasted_iota(jnp.int32, (bk, HD_SB), 0) // dec_seq
    q_rows = []
    for h in range(H_SB):
        qh = jnp.concatenate([q_ref[:, h * HD_SB:(h + 1) * HD_SB] * SB_SCALE] * (bk // dec_seq), axis=0)
        q_rows.append(jnp.where(rid == h, qh, 0.0).astype(BF16))

    def logits(pages):
        zp = None
        for h in range(H_SB):
            kh = jnp.concatenate([k_refs[p][pl.ds(h, bk, stride=H_SB), :] for p in pages], axis=0)
            part = lax.dot_general(kh.astype(BF16), q_rows[h], (((1,), (1,)), ((), ())),
                                   preferred_element_type=F32)
            zp = part if zp is None else zp + part
        return [zp[i * bk:(i + 1) * bk].T[:rows] + bias for i in range(len(pages))]

    groups = [list(range(g, g + page_group)) for g in range(0, npg, page_group)]
    carry, acc = carry_ref[...], acc_ref[...]
    zs_next = logits(groups[0])
    for gi, pages in enumerate(groups):
        zs = zs_next
        if gi + 1 < len(groups):
            zs_next = logits(groups[gi + 1])
        get_vs = [lambda h, p=p: v_refs[p][pl.ds(h, bk, stride=H_SB), :] for p in pages]
        carry, acc = attend(zs, get_vs, [None] * len(pages), carry, acc)
    carry_ref[...] = carry
    acc_ref[...] = acc

    @pl.when(step == pl.num_programs(1) - 1)
    def _():
        for h in range(H_SB):
            sl = slice(h * HD_SB, (h + 1) * HD_SB)
            o_ref[:, sl] = head_rows(acc_ref, h) * _silu(zg_ref[:, sl])


def _sb_sample(proj, cache_k, cache_v, page_table, sb_bias, layer, dec_seq, pages_per_step=16, page_group=4):
    m = proj.shape[0]
    nb = m // dec_seq
    n_pages = page_table.shape[1]
    npg = pages_per_step
    assert n_pages % npg == 0 and PAGE_SIZE % dec_seq == 0
    cb = lambda col: col // BRANCH_W
    page = lambda p: (lambda b, s, pt: (layer, pt[b, n_pages - 1 - (s * npg + p)], 0, 0))
    row = lambda col: pl.BlockSpec((dec_seq, BRANCH_W), lambda b, s, pt, col=col: (b, cb(col)))
    rows = H_SB * dec_seq
    bias_tile = jnp.broadcast_to(jnp.repeat(sb_bias * LOG2E, dec_seq)[:, None], (rows, PAGE_SIZE))
    page_spec = lambda p: pl.BlockSpec((None, None, PAGE_SIZE * H_SB, HD_SB), page(p))
    grid_spec = pltpu.PrefetchScalarGridSpec(
        num_scalar_prefetch=1,
        grid=(nb, n_pages // npg),
        in_specs=[pl.BlockSpec((rows, PAGE_SIZE), lambda b, s, pt: (0, 0)),
                  row(COL_B_Q), row(COL_B_K), row(COL_B_V), row(COL_B_Z)]
                 + [page_spec(p) for p in range(npg)] + [page_spec(p) for p in range(npg)],
        out_specs=pl.BlockSpec((dec_seq, BRANCH_W), lambda b, s, pt: (b, 0)),
        scratch_shapes=[pltpu.VMEM((rows, HD_SB), F32), pltpu.VMEM((rows, 1), F32)],
    )
    return pl.pallas_call(
        functools.partial(_sb_sample_kernel, dec_seq=dec_seq, pages_per_step=npg, page_group=page_group),
        grid_spec=grid_spec,
        out_shape=jax.ShapeDtypeStruct((m, BRANCH_W), F32),
        compiler_params=_params("arbitrary", "arbitrary"),
        name="sb_sample",
    )(page_table, bias_tile, proj, proj, proj, proj, *([cache_k] * npg), *([cache_v] * npg))


def _softplus(x):
    return jnp.maximum(x, 0.0) + jnp.log(1.0 + jnp.exp(-jnp.abs(x)))


def _dot_exact_rhs(a_f32, sel_bf16):
    hi, mid, lo = _split3(a_f32)
    d = lambda p: jnp.dot(p, sel_bf16, preferred_element_type=F32)
    return d(hi) + (d(mid) + d(lo))


def _dot_exact_lhs(sel_bf16, b_f32):
    hi, mid, lo = _split3(b_f32)
    d = lambda p: jnp.dot(sel_bf16, p, preferred_element_type=F32)
    return d(hi) + (d(mid) + d(lo))


def _dot_x3(a, b):
    ah, al = _split2(a)
    bh, bl = _split2(b)
    d = lambda x, y: jnp.dot(x, y, preferred_element_type=F32)
    return d(ah, bh) + (d(ah, bl) + d(al, bh))


def _dn_prep_tail(seg_inputs, w_ref, ps_ref, alog_ref, dtb_ref, eg_ref, eb_ref,
                  qn_ref, kn_ref, vc_ref, g_ref, beta_ref):
    outs = (qn_ref, kn_ref, vc_ref)
    for seg, (x0, x1, x2, x3) in enumerate(seg_inputs):
        sl = slice(seg * BRANCH_W, (seg + 1) * BRANCH_W)
        w = w_ref[:, sl]
        y = _silu(((x3 * w[0:1] + x2 * w[1:2]) + x1 * w[2:3]) + x0 * w[3:4])
        if seg < 2:
            scale = DK_DN ** -0.5 if seg == 0 else 1.0
            for h in range(H_DN):
                hs = slice(h * DK_DN, (h + 1) * DK_DN)
                yh = y[:, hs]
                ss = jnp.sum(yh * yh, axis=-1, keepdims=True)
                outs[seg][:, hs] = yh * lax.rsqrt(ss + RMS_EPS) * scale
        else:
            outs[seg][...] = y
    ps = ps_ref[...]
    g = -jnp.exp(alog_ref[...]) * _softplus(ps + dtb_ref[...])
    g_ref[...] = _dot_exact_rhs(g, eg_ref[...])
    beta_ref[...] = _dot_exact_rhs(_sigmoid(ps), eb_ref[...])


def _dn_prep_prompt_kernel(xq_ref, xk_ref, xv_ref, hq_ref, hk_ref, hv_ref, *rest):
    first = pl.program_id(0) == 0
    segs = []
    for x_ref, h_ref in ((xq_ref, hq_ref), (xk_ref, hk_ref), (xv_ref, hv_ref)):
        x = x_ref[...]
        halo = jnp.where(first, 0.0, h_ref[...])
        rid = lax.broadcasted_iota(jnp.int32, halo.shape, 0)
        shifted = []
        for k in (1, 2, 3):
            xr = pltpu.roll(x, k, 0)
            head = jnp.where(rid < k, pltpu.roll(halo, k, 0), xr[:8])
            shifted.append(jnp.concatenate([head, xr[8:]], axis=0))
        segs.append((x, *shifted))
    _dn_prep_tail(segs, *rest)


def _dn_prep_sample_kernel(xq_ref, xk_ref, xv_ref, *rest):
    shifted = rest[:9]
    segs = []
    for s, x_ref in enumerate((xq_ref, xk_ref, xv_ref)):
        segs.append((x_ref[...], shifted[s][...], shifted[3 + s][...], shifted[6 + s][...]))
    _dn_prep_tail(segs, *rest[9:])


def _dn_prep(proj, proj_small, shifted, conv_w, a_log, dt_bias, tm=256):
    m = proj.shape[0]
    cb = lambda col: col // BRANCH_W
    xblk = lambda s: pl.BlockSpec((tm, BRANCH_W), lambda i, s=s: (i, cb(COL_C_QKV) + s))
    in_specs = [xblk(0), xblk(1), xblk(2)]
    args = [proj, proj, proj]
    if shifted is None:
        hblk = lambda s: pl.BlockSpec(
            (8, BRANCH_W), lambda i, s=s: (jnp.maximum(i * (tm // 8) - 1, 0), cb(COL_C_QKV) + s))
        in_specs += [hblk(0), hblk(1), hblk(2)]
        args += [proj, proj, proj]
        body = _dn_prep_prompt_kernel
    else:
        for arr in shifted:
            for s in range(3):
                in_specs.append(pl.BlockSpec((tm, BRANCH_W), lambda i, s=s: (i, s)))
                args.append(arr)
        body = _dn_prep_sample_kernel
    lane = lax.broadcasted_iota(jnp.int32, (LANES, BRANCH_W), 0)
    head = lax.broadcasted_iota(jnp.int32, (LANES, BRANCH_W), 1) // DV_DN
    e_g = (lane == head).astype(BF16)
    e_b = (lane == head + H_DN).astype(BF16)
    pad = lambda v, off: jnp.zeros((1, LANES), F32).at[0, off:off + H_DN].set(v)
    const = lambda shape: pl.BlockSpec(shape, lambda i: (0, 0))
    in_specs += [const((CONV_W, D_DN_CONV)), pl.BlockSpec((tm, LANES), lambda i: (i, 0)),
                 const((1, LANES)), const((1, LANES)), const((LANES, BRANCH_W)), const((LANES, BRANCH_W))]
    args += [conv_w, proj_small, pad(a_log, 0), pad(dt_bias, 0), e_g, e_b]
    out = pl.BlockSpec((tm, BRANCH_W), lambda i: (i, 0))
    return pl.pallas_call(
        body,
        grid=(m // tm,),
        in_specs=in_specs,
        out_specs=[out] * 5,
        out_shape=[jax.ShapeDtypeStruct((m, BRANCH_W), F32)] * 5,
        compiler_params=_params("arbitrary"),
        name="dn_prep",
    )(*args)


def _dn_masks(c, block):
    i = lax.broadcasted_iota(jnp.int32, (c, c), 0)
    j = lax.broadcasted_iota(jnp.int32, (c, c), 1)
    same = (i // block) == (j // block)
    return dict(incl=same & (i >= j), strict=same & (i > j), incl_t=same & (i <= j), same=same,
                eye=(i == j))


def _dn_chunk_core(qs, ks, vs, g_reps, beta_reps, masks, n_double):
    c = qs[0].shape[0]
    each = lambda f, *lists: [f(*args) for args in zip(*lists)]
    onef = lambda m: jnp.where(m, 1.0, 0.0).astype(BF16)
    incl = masks["incl"]
    incl_b, same_b, ones_b = onef(incl), onef(masks["same"]), jnp.ones((c, c), BF16)
    contract_last = lambda x, y: lax.dot_general(x, y, (((1,), (1,)), ((), ())), preferred_element_type=F32)
    gcums = each(lambda g: _dot_exact_lhs(incl_b, g), g_reps)
    glasts = each(lambda g: _dot_exact_lhs(same_b, g), g_reps)
    g_rows = each(lambda g: _dot_exact_lhs(ones_b, jnp.where(masks["incl_t"], g[:, :c], 0.0)), g_reps)
    decays = each(lambda gc, gr: jnp.where(incl, jnp.exp(jnp.where(incl, gc[:, :c] - gr, 0.0)), 0.0),
                  gcums, g_rows)
    kbs = each(lambda k: k.astype(BF16), ks)
    kks = each(lambda kb: contract_last(kb, kb), kbs)
    qks = each(lambda q, kb, d: contract_last(q.astype(BF16), kb) * d, qs, kbs, decays)
    eye = jnp.where(masks["eye"], 1.0, 0.0)
    ps = each(lambda b, kk, d: jnp.where(masks["strict"], -(b[:, :c] * kk * d), 0.0), beta_reps, kks, decays)
    xs = each(lambda p: eye + p, ps)
    for _ in range(n_double):
        ps = each(lambda p: _dot_x3(p, p), ps)
        xs = each(lambda x, p: x + _dot_x3(x, p), xs, ps)
    gams = each(jnp.exp, gcums)
    u_parts = each(lambda x, b, v: _dot_x3(x, b * v), xs, beta_reps, vs)
    w_parts = each(lambda x, b, gam, k: _dot_x3(x, (b * gam) * k), xs, beta_reps, gams, ks)
    q_gs = each(lambda q, gam: q * gam, qs, gams)
    k_ends = each(lambda k, gl, gc: k * jnp.exp(gl - gc), ks, glasts, gcums)
    g_ends = each(jnp.exp, glasts)
    return u_parts, w_parts, qks, q_gs, k_ends, g_ends


def _gated_rms(o, norm_g, z):
    ms = jnp.mean(o * o, axis=-1, keepdims=True)
    return o * lax.rsqrt(ms + RMS_EPS) * norm_g * _silu(z)


def _dn_prompt_kernel(q_ref, k_ref, v_ref, g_ref, b_ref, z_ref, ng_ref, o_ref, sfin_ref, s_ref):
    n = pl.program_id(0)

    @pl.when(n == 0)
    def _():
        s_ref[...] = jnp.zeros_like(s_ref)

    c = q_ref.shape[0]
    masks = _dn_masks(c, c)
    heads = range(H_DN)
    cols = [slice(h * DV_DN, (h + 1) * DV_DN) for h in heads]
    take = lambda ref: [ref[:, hs] for hs in cols]
    u_ps, w_ps, qks, q_gs, k_ends, g_ends = _dn_chunk_core(
        take(q_ref), take(k_ref), take(v_ref), take(g_ref), take(b_ref), masks, int(math.log2(c)) - 1)
    dot = lambda x, y: jnp.dot(x, y, preferred_element_type=F32)
    ss = [s_ref[h] for h in heads]
    sbs = [s.astype(BF16) for s in ss]
    ubs = [(u_ps[h] - dot(w_ps[h].astype(BF16), sbs[h])).astype(BF16) for h in heads]
    os_ = [dot(q_gs[h].astype(BF16), sbs[h]) + dot(qks[h].astype(BF16), ubs[h]) for h in heads]
    for h in heads:
        s_ref[h] = g_ends[h][0:1, :] * ss[h] + lax.dot_general(
            k_ends[h].astype(BF16), ubs[h], (((0,), (0,)), ((), ())), preferred_element_type=F32)
        o_ref[:, cols[h]] = _gated_rms(os_[h], ng_ref[...], z_ref[:, cols[h]]).astype(o_ref.dtype)

    @pl.when(n == pl.num_programs(0) - 1)
    def _():
        sfin_ref[...] = s_ref[...]


def _dn_prompt(qn, kn, vc, g_rep, beta_rep, proj, norm_g):
    m = qn.shape[0]
    c = DN_CHUNK
    blk = pl.BlockSpec((c, BRANCH_W), lambda n: (n, 0))
    zblk = pl.BlockSpec((c, BRANCH_W), lambda n: (n, COL_C_Z // BRANCH_W))
    state = pl.BlockSpec((H_DN, DK_DN, DV_DN), lambda n: (0, 0, 0))
    return pl.pallas_call(
        _dn_prompt_kernel,
        grid=(m // c,),
        in_specs=[blk, blk, blk, blk, blk, zblk, pl.BlockSpec((1, DV_DN), lambda n: (0, 0))],
        out_specs=[blk, state],
        out_shape=[jax.ShapeDtypeStruct((m, BRANCH_W), BF16),
                   jax.ShapeDtypeStruct((H_DN, DK_DN, DV_DN), F32)],
        scratch_shapes=[pltpu.VMEM((H_DN, DK_DN, DV_DN), F32)],
        compiler_params=_params("arbitrary"),
        name="dn_prompt",
    )(qn, kn, vc, g_rep, beta_rep, proj, norm_g.reshape(1, DV_DN))


def _dn_sample_kernel(q_ref, k_ref, v_ref, g_ref, b_ref, z_ref, ng_ref, s0_ref, o_ref, snew_ref, *, dec_seq, n_seq):
    t = dec_seq
    rows = H_DN * t
    seqs, heads = range(n_seq), range(H_DN)
    cols = [slice(h * DV_DN, (h + 1) * DV_DN) for h in heads]
    srows = [slice(i * t, (i + 1) * t) for i in seqs]
    take = lambda ref: [jnp.concatenate([ref[srows[i], hs] for hs in cols], axis=0) for i in seqs]
    head_rows = lambda x, h: x[h * t:(h + 1) * t]
    masks = _dn_masks(rows, t)
    u_ps, w_ps, qks, q_gs, k_ends, g_ends = _dn_chunk_core(
        take(q_ref), take(k_ref), take(v_ref), take(g_ref), take(b_ref), masks, max(int(math.log2(t)) - 1, 0))
    dot = lambda x, y: jnp.dot(x, y, preferred_element_type=F32)
    sbs = [[s0_ref[i, h].astype(BF16) for h in heads] for i in seqs]
    ubs = [jnp.concatenate([head_rows(u_ps[i], h) - dot(head_rows(w_ps[i], h).astype(BF16), sbs[i][h])
                            for h in heads], axis=0).astype(BF16) for i in seqs]
    os_ = [jnp.concatenate([dot(head_rows(q_gs[i], h).astype(BF16), sbs[i][h]) for h in heads], axis=0)
           + dot(qks[i].astype(BF16), ubs[i]) for i in seqs]
    rid = lax.broadcasted_iota(jnp.int32, (rows, DK_DN), 0) // t
    for i in seqs:
        for h in heads:
            kh = jnp.where(rid == h, k_ends[i], 0.0).astype(BF16)
            upd = lax.dot_general(kh, ubs[i], (((0,), (0,)), ((), ())), preferred_element_type=F32)
            snew_ref[i, h] = head_rows(g_ends[i], h)[0:1, :] * s0_ref[i, h] + upd
            o_ref[srows[i], cols[h]] = _gated_rms(head_rows(os_[i], h), ng_ref[...],
                                                  z_ref[srows[i], cols[h]]).astype(o_ref.dtype)


def _dn_sample(qn, kn, vc, g_rep, beta_rep, proj, norm_g, s0, layer, dec_seq, n_seq=4):
    m = qn.shape[0]
    nb = m // dec_seq
    assert nb % n_seq == 0
    blk = pl.BlockSpec((n_seq * dec_seq, BRANCH_W), lambda b: (b, 0))
    zblk = pl.BlockSpec((n_seq * dec_seq, BRANCH_W), lambda b: (b, COL_C_Z // BRANCH_W))
    return pl.pallas_call(
        functools.partial(_dn_sample_kernel, dec_seq=dec_seq, n_seq=n_seq),
        grid=(nb // n_seq,),
        in_specs=[blk, blk, blk, blk, blk, zblk, pl.BlockSpec((1, DV_DN), lambda b: (0, 0)),
                  pl.BlockSpec((None, n_seq, H_DN, DK_DN, DV_DN), lambda b: (layer, b, 0, 0, 0))],
        out_specs=[blk, pl.BlockSpec((n_seq, H_DN, DK_DN, DV_DN), lambda b: (b, 0, 0, 0))],
        out_shape=[jax.ShapeDtypeStruct((m, BRANCH_W), BF16),
                   jax.ShapeDtypeStruct((nb, H_DN, DK_DN, DV_DN), F32)],
        compiler_params=_params("arbitrary"),
        name="dn_sample",
    )(qn, kn, vc, g_rep, beta_rep, proj, norm_g.reshape(1, DV_DN), s0)


def _mixer_layer(x, p, layer, sample, tm):
    m = x.shape[0]
    proj = _matmul(x, p["w_in"], layer, N_MAIN, tm, 1024, "proj_main")
    gates = _matmul(x, p["w_gate"], layer, N_BRANCH * D_MODEL, tm, 1024, "proj_gate")
    proj_small = _matmul(x, p["w_small"], layer, LANES, tm, LANES, "proj_small")

    if sample is None:
        w_mix, bias_full = p["sg_w"], p["sg_bias_prompt"]
    else:
        w_mix, bias_full = p["sg_w_sample"], p["sg_bias_sample"]
    o_a, sg_v = _spatial_gating(proj, w_mix, bias_full, p["sg_ln_g"], p["sg_ln_b"])

    if sample is None:
        o_b, k_rows, v_rows = _sb_prompt(proj, p["sb_bias"])
        qn, kn, vc, g_rep, beta_rep = _dn_prep(proj, proj_small, None, p["dn_conv_w"], p["dn_a_log"],
                                               p["dn_dt_bias"])
        o_c, s_new = _dn_prompt(qn, kn, vc, g_rep, beta_rep, proj, p["dn_norm_g"])
        conv_new = proj[m - (CONV_W - 1):, COL_C_QKV:COL_C_QKV + D_DN_CONV][None]
        s_new = s_new[None]
    else:
        t = sample["dec_seq"]
        nb = m // t
        o_b = _sb_sample(proj, sample["cache_k"], sample["cache_v"], sample["page_table"], p["sb_bias"],
                         layer, t)
        k_rows = proj[:, COL_B_K:COL_B_K + BRANCH_W]
        v_rows = proj[:, COL_B_V:COL_B_V + BRANCH_W]
        conv_in = proj[:, COL_C_QKV:COL_C_QKV + D_DN_CONV]
        xp = jnp.concatenate([sample["conv"], conv_in.reshape(nb, t, D_DN_CONV)], axis=1)
        shifted = [xp[:, CONV_W - 1 - k:CONV_W - 1 - k + t].reshape(m, D_DN_CONV) for k in (1, 2, 3)]
        qn, kn, vc, g_rep, beta_rep = _dn_prep(proj, proj_small, shifted, p["dn_conv_w"], p["dn_a_log"],
                                               p["dn_dt_bias"])
        o_c, s_new = _dn_sample(qn, kn, vc, g_rep, beta_rep, proj, p["dn_norm_g"], sample["ssm"], layer, t)
        conv_new = xp[:, t:]

    x_new = _merge_out(o_a, o_b, o_c, gates, p["w_branch"], p["gate_b"], p["w_out"], x, p["ln_g"], p["ln_b"])
    return x_new, k_rows, v_rows, conv_new, s_new, sg_v


def kernel(x_prompt, x_sample, cache_sb_k, cache_sb_v, state_dn_conv, state_dn_ssm, page_table, ln_in_g, ln_in_b,
           w_in, sg_ln_g, sg_ln_b, sg_w, sg_b, sb_bias, dn_conv_w, dn_a_log, dn_dt_bias, dn_norm_g, gate_b,
           w_branch, w_out, ln_g, ln_b):
    bp, seq, d = x_prompt.shape
    bs, dec_seq, _ = x_sample.shape
    assert bp == 1, "prompt rows are treated as one sequence"
    depth = w_in.shape[0]
    n_pool = cache_sb_k.shape[1]

    xp = _layer_norm(x_prompt.reshape(bp * seq, d), ln_in_g, ln_in_b)
    xs = _layer_norm(x_sample.reshape(bs * dec_seq, d), ln_in_g, ln_in_b)
    cache_k = cache_sb_k.reshape(depth, n_pool, PAGE_SIZE * H_SB, HD_SB)
    cache_v = cache_sb_v.reshape(depth, n_pool, PAGE_SIZE * H_SB, HD_SB)

    w_gate = w_in[:, :, COL_GATE:].astype(BF16)
    w_small = jnp.pad(w_in[:, :, COL_SMALL:COL_GATE], ((0, 0), (0, 0), (0, LANES - 2 * H_DN))).astype(BF16)
    w_branch_b = w_branch.astype(BF16)
    w_out_b = w_out.astype(BF16)

    reps = SG_CHUNK // dec_seq
    eye = jnp.eye(reps, dtype=F32)
    outs = [[] for _ in range(9)]
    for l in range(depth):
        sg_small = jnp.tril(sg_w[l][:, :dec_seq, :dec_seq])
        p = {
            "w_in": w_in, "w_gate": w_gate, "w_small": w_small,
            "sg_w": sg_w[l],
            "sg_bias_prompt": jnp.repeat(sg_b[l].T, BRANCH_W // SG_GROUPS, axis=1),
            "sg_w_sample": jax.vmap(lambda w: jnp.kron(eye, w))(sg_small),
            "sg_bias_sample": jnp.repeat(jnp.tile(sg_b[l][:, :dec_seq].T, (reps, 1)),
                                         BRANCH_W // SG_GROUPS, axis=1),
            "sg_ln_g": sg_ln_g[l], "sg_ln_b": sg_ln_b[l], "sb_bias": sb_bias[l],
            "dn_conv_w": dn_conv_w[l], "dn_a_log": dn_a_log[l], "dn_dt_bias": dn_dt_bias[l],
            "dn_norm_g": dn_norm_g[l], "gate_b": gate_b[l],
            "w_branch": w_branch_b[l], "w_out": w_out_b[l],
            "ln_g": ln_g[l], "ln_b": ln_b[l],
        }
        xp, kp, vp, cp, sp, _ = _mixer_layer(xp, p, l, None, tm=1024)
        sample = {"cache_k": cache_k, "cache_v": cache_v, "page_table": page_table,
                  "conv": state_dn_conv[l], "ssm": state_dn_ssm, "dec_seq": dec_seq}
        xs, ks, vs, cs, ss, sgv = _mixer_layer(xs, p, l, sample, tm=1024)
        hd = (H_SB, HD_SB)
        for lst, val in zip(outs, (kp.reshape(bp, seq, *hd), vp.reshape(bp, seq, *hd),
                                   ks.reshape(bs, dec_seq, *hd), vs.reshape(bs, dec_seq, *hd),
                                   cp, cs, sp, ss, sgv.reshape(bs, dec_seq, BRANCH_W))):
            lst.append(val)
    return (xp.reshape(bp, seq, d), xs.reshape(bs, dec_seq, d), *[jnp.stack(o) for o in outs])
```

```python
import functools
import math

import jax
import jax.numpy as jnp
from jax import lax
from jax.experimental import pallas as pl
from jax.experimental.pallas import tpu as pltpu

F32 = jnp.float32
BF16 = jnp.bfloat16

VMEM_LIMIT_BYTES = 56 * 1024 * 1024
LANES = 128

D_MODEL = 2048
BRANCH_W = 1024
N_BRANCH = 3
SG_GROUPS = 8
SG_CHUNK = 128
H_SB = 8
HD_SB = 128
H_DN = 8
DK_DN = 128
DV_DN = 128
D_DN_CONV = 3 * BRANCH_W
CONV_W = 4
DN_CHUNK = 64
PAGE_SIZE = 128
LN_EPS = 1e-5
RMS_EPS = 1e-6
DEPTH = 2
ALPHA = (2 * DEPTH) ** 0.25

COL_A_U, COL_A_V, COL_A_Z = 0, 1024, 2048
COL_B_Q, COL_B_K, COL_B_V, COL_B_Z = 3072, 4096, 5120, 6144
COL_C_QKV, COL_C_Z = 7168, 10240
N_MAIN = 11264
COL_SMALL = 11264
COL_GATE = COL_SMALL + 2 * H_DN
COL_G_SMALL = N_BRANCH * D_MODEL
N_GATE = COL_G_SMALL + LANES
LOG2E = 1.4426950408889634


def _params(*sem):
    return pltpu.CompilerParams(dimension_semantics=sem, vmem_limit_bytes=VMEM_LIMIT_BYTES)


def _silu(x):
    return x * (1.0 / (1.0 + jnp.exp(-x)))


def _sigmoid(x):
    return 1.0 / (1.0 + jnp.exp(-x))


def _gelu(x):
    c = math.sqrt(2.0 / math.pi)
    return 0.5 * x * (1.0 + jnp.tanh(c * (x + 0.044715 * (x * x * x))))


def _layer_norm_rows(x, g, b):
    mu = jnp.mean(x, axis=-1, keepdims=True)
    xc = x - mu
    var = jnp.mean(xc * xc, axis=-1, keepdims=True)
    return xc * lax.rsqrt(var + LN_EPS) * g + b


def _ln_kernel(x_ref, g_ref, b_ref, o_ref):
    o_ref[...] = _layer_norm_rows(x_ref[...], g_ref[...], b_ref[...])


def _layer_norm(x, g, b, tm=512):
    m, d = x.shape
    tm = min(tm, m)
    return pl.pallas_call(
        _ln_kernel,
        grid=(m // tm,),
        in_specs=[pl.BlockSpec((tm, d), lambda i: (i, 0)),
                  pl.BlockSpec((1, d), lambda i: (0, 0)),
                  pl.BlockSpec((1, d), lambda i: (0, 0))],
        out_specs=pl.BlockSpec((tm, d), lambda i: (i, 0)),
        out_shape=jax.ShapeDtypeStruct((m, d), F32),
        compiler_params=_params("arbitrary"),
        name="ln_in",
    )(x, g.reshape(1, d), b.reshape(1, d))


def _mm_kernel(x_ref, w_ref, o_ref, xb_ref):
    @pl.when(pl.program_id(1) == 0)
    def _():
        xb_ref[...] = x_ref[...].astype(BF16)

    o_ref[...] = lax.dot_general(xb_ref[...], w_ref[...].astype(BF16), (((1,), (1,)), ((), ())),
                                 preferred_element_type=F32)


def _matmul(x, w_t, layer, n, tm, tn, name):
    m, k = x.shape
    tm = min(tm, m)
    return pl.pallas_call(
        _mm_kernel,
        grid=(m // tm, n // tn),
        in_specs=[pl.BlockSpec((tm, k), lambda i, j: (i, 0)),
                  pl.BlockSpec((None, tn, k), lambda i, j: (layer, j, 0))],
        out_specs=pl.BlockSpec((tm, tn), lambda i, j: (i, j)),
        out_shape=jax.ShapeDtypeStruct((m, n), F32),
        scratch_shapes=[pltpu.VMEM((tm, k), BF16)],
        compiler_params=_params("arbitrary", "arbitrary"),
        name=name,
    )(x, w_t)


def _sg_kernel(u_ref, v_ref, z_ref, w_ref, bias_ref, g_ref, b_ref, o_ref, vn_ref):
    u = _gelu(u_ref[...])
    vn = _layer_norm_rows(_gelu(v_ref[...]), g_ref[...], b_ref[...])
    vn_ref[...] = vn
    vb = vn.astype(BF16)
    rows = vn.shape[0]
    r = lax.broadcasted_iota(jnp.int32, (rows, rows), 0)
    c = lax.broadcasted_iota(jnp.int32, (rows, rows), 1)
    lower = c <= r
    parts = []
    for grp in range(SG_GROUPS):
        w = jnp.where(lower, w_ref[grp], 0.0).astype(BF16)
        parts.append(jnp.dot(w, vb[:, grp * LANES:(grp + 1) * LANES], preferred_element_type=F32))
    mixed = jnp.concatenate(parts, axis=-1) + bias_ref[...]
    o_ref[...] = (u * mixed * _silu(z_ref[...])).astype(o_ref.dtype)


def _spatial_gating(proj, w_mix, bias_full, ln_g, ln_b):
    m = proj.shape[0]
    tm = SG_CHUNK
    blk = lambda col: pl.BlockSpec((tm, BRANCH_W), lambda i, col=col: (i, col // BRANCH_W))
    return pl.pallas_call(
        _sg_kernel,
        grid=(m // tm,),
        in_specs=[blk(COL_A_U), blk(COL_A_V), blk(COL_A_Z),
                  pl.BlockSpec((SG_GROUPS, tm, tm), lambda i: (0, 0, 0)),
                  pl.BlockSpec((tm, BRANCH_W), lambda i: (0, 0)),
                  pl.BlockSpec((1, BRANCH_W), lambda i: (0, 0)),
                  pl.BlockSpec((1, BRANCH_W), lambda i: (0, 0))],
        out_specs=[pl.BlockSpec((tm, BRANCH_W), lambda i: (i, 0)),
                   pl.BlockSpec((tm, BRANCH_W), lambda i: (i, 0))],
        out_shape=[jax.ShapeDtypeStruct((m, BRANCH_W), BF16),
                   jax.ShapeDtypeStruct((m, BRANCH_W), F32)],
        compiler_params=_params("arbitrary"),
        name="spatial_gating",
    )(proj, proj, proj, w_mix, bias_full, ln_g.reshape(1, -1), ln_b.reshape(1, -1))


def _merge_out_kernel(oa_ref, ob_ref, oc_ref, wb_ref, ga_ref, gb_ref, gc_ref, gbias_ref, wo_ref, x_ref,
                      g_ref, b_ref, o_ref):
    merged = None
    for n, (o_n, gate_ref) in enumerate(((oa_ref, ga_ref), (ob_ref, gb_ref), (oc_ref, gc_ref))):
        y = jnp.dot(o_n[...].astype(BF16), wb_ref[n], preferred_element_type=F32)
        t = _sigmoid(gate_ref[...] + gbias_ref[n:n + 1, :]) * y
        merged = t if merged is None else merged + t
    out = jnp.dot(merged.astype(BF16), wo_ref[...], preferred_element_type=F32)
    o_ref[...] = _layer_norm_rows(ALPHA * x_ref[...] + out, g_ref[...], b_ref[...])


def _merge_out(o_a, o_b, o_c, gates, w_branch, gate_b, w_out, x, ln_g, ln_b, tm=256):
    m, d = x.shape
    tm = min(tm, m)
    resident = lambda shape: pl.BlockSpec(shape, lambda i: (0,) * len(shape), pipeline_mode=pl.Buffered(1))
    oblk = pl.BlockSpec((tm, BRANCH_W), lambda i: (i, 0))
    gblk = lambda n: pl.BlockSpec((tm, d), lambda i, n=n: (i, n))
    row = pl.BlockSpec((tm, d), lambda i: (i, 0))
    vec = pl.BlockSpec((1, d), lambda i: (0, 0))
    return pl.pallas_call(
        _merge_out_kernel,
        grid=(m // tm,),
        in_specs=[oblk, oblk, oblk, resident((N_BRANCH, BRANCH_W, d)), gblk(0), gblk(1), gblk(2),
                  pl.BlockSpec((N_BRANCH, d), lambda i: (0, 0)), resident((d, d)), row, vec, vec],
        out_specs=row,
        out_shape=jax.ShapeDtypeStruct((m, d), F32),
        compiler_params=_params("arbitrary"),
        name="merge_out",
    )(o_a, o_b, o_c, w_branch, gates, gates, gates, gate_b, w_out, x, ln_g.reshape(1, d), ln_b.reshape(1, d))


SB_SCALE = HD_SB ** -0.5 * LOG2E


def _split2(x):
    hi = x.astype(BF16)
    lo = (x - hi.astype(F32)).astype(BF16)
    return hi, lo


def _split3(x):
    hi = x.astype(BF16)
    r = x - hi.astype(F32)
    mid = r.astype(BF16)
    lo = (r - mid.astype(F32)).astype(BF16)
    return hi, mid, lo


def _suffix_matrix(bk):
    j = lax.broadcasted_iota(jnp.int32, (2 * bk, bk), 0) % bk
    s = lax.broadcasted_iota(jnp.int32, (2 * bk, bk), 1)
    return jnp.where(j > s, 1.0, 0.0).astype(BF16)


def _sb_weights(zs, spent, suffix, masks):
    sign = jnp.uint32(0x80000000)
    log_betas, costs, pieces = [], [], []
    for z, mask in zip(zs, masks):
        neg_abs = lax.bitcast_convert_type(lax.bitcast_convert_type(z, jnp.uint32) | sign, F32)
        cost = jnp.maximum(z, 0.0) + jnp.log2(1.0 + jnp.exp2(neg_abs))
        log_betas.append(z - cost)
        if mask is not None:
            cost = jnp.where(mask, cost, 0.0)
        costs.append(cost)
        pieces.append(jnp.concatenate(_split2(cost), axis=1))
    withins = [jnp.dot(piece, suffix, preferred_element_type=F32) for piece in pieces]
    weights = []
    for log_beta, cost, within, mask in zip(log_betas, costs, withins, masks):
        a = jnp.exp2(log_beta - (spent + within))
        weights.append(a if mask is None else jnp.where(mask, a, 0.0))
        spent = spent + jnp.sum(cost, axis=-1, keepdims=True)
    return weights, spent


def _sb_prompt_kernel(bias_ref, q_ref, k_ref, v_ref, zg_ref, o_ref, ko_ref, vo_ref, kb_ref, vb_ref, *, bq, bk,
                      unroll):
    h = pl.program_id(0)
    qi = pl.program_id(1)
    nsub = bq // bk

    @pl.when(qi == 0)
    def _():
        k = k_ref[...]
        v = v_ref[...]
        ko_ref[...] = k
        vo_ref[...] = v
        lane = lax.broadcasted_iota(jnp.int32, k.shape, 1)
        kb_ref[:, :HD_SB] = k.astype(BF16)
        kb_ref[:, HD_SB:] = jnp.where(lane < 3, 1.0, 0.0).astype(BF16)
        vb_ref[...] = v.astype(BF16)

    q = q_ref[...]
    lane = lax.broadcasted_iota(jnp.int32, q.shape, 1)
    b_hi, b_mid, b_lo = (piece.astype(F32) for piece in _split3(jnp.full(q.shape, bias_ref[h] * LOG2E, F32)))
    q_bias = jnp.where(lane == 0, b_hi, jnp.where(lane == 1, b_mid, jnp.where(lane == 2, b_lo, 0.0)))
    qb = jnp.concatenate([(q * SB_SCALE).astype(BF16), q_bias.astype(BF16)], axis=1)
    suffix = _suffix_matrix(bk)
    q_pos = qi * bq + lax.broadcasted_iota(jnp.int32, (bq, bk), 0)
    k_off = lax.broadcasted_iota(jnp.int32, (bq, bk), 1)

    def steps(js, carry, acc, masked):
        starts = [pl.multiple_of(j * bk, bk) for j in js]
        zs = [lax.dot_general(qb, kb_ref[pl.ds(st, bk), :], (((1,), (1,)), ((), ())),
                              preferred_element_type=F32) for st in starts]
        masks = [(j * bk + k_off) < q_pos if masked else None for j in js]
        weights, carry = _sb_weights(zs, carry, suffix, masks)
        for a, st in zip(weights, starts):
            acc = acc + jnp.dot(a.astype(BF16), vb_ref[pl.ds(st, bk), :], preferred_element_type=F32)
        return carry, acc

    carry = jnp.zeros((bq, 1), F32)
    acc = jnp.zeros((bq, HD_SB), F32)
    for t in range(nsub // unroll):
        first = qi * nsub + nsub - 1 - t * unroll
        carry, acc = steps([first - u for u in range(unroll)], carry, acc, True)

    def body(t, ca):
        first = qi * nsub - 1 - t * unroll
        return steps([first - u for u in range(unroll)], ca[0], ca[1], False)

    carry, acc = lax.fori_loop(0, qi * (nsub // unroll), body, (carry, acc))
    o_ref[...] = (acc * _silu(zg_ref[...])).astype(o_ref.dtype)


def _sb_prompt(proj, sb_bias, bq=1024, bk=256, unroll=2):
    m = proj.shape[0]
    assert (bq // bk) % unroll == 0
    cb = lambda col: col // HD_SB
    return pl.pallas_call(
        functools.partial(_sb_prompt_kernel, bq=bq, bk=bk, unroll=unroll),
        grid=(H_SB, m // bq),
        in_specs=[pl.BlockSpec(memory_space=pltpu.SMEM),
                  pl.BlockSpec((bq, HD_SB), lambda h, i: (i, cb(COL_B_Q) + h)),
                  pl.BlockSpec((m, HD_SB), lambda h, i: (0, cb(COL_B_K) + h)),
                  pl.BlockSpec((m, HD_SB), lambda h, i: (0, cb(COL_B_V) + h)),
                  pl.BlockSpec((bq, HD_SB), lambda h, i: (i, cb(COL_B_Z) + h))],
        out_specs=[pl.BlockSpec((bq, HD_SB), lambda h, i: (i, h)),
                   pl.BlockSpec((m, HD_SB), lambda h, i: (0, h)),
                   pl.BlockSpec((m, HD_SB), lambda h, i: (0, h))],
        out_shape=[jax.ShapeDtypeStruct((m, BRANCH_W), BF16),
                   jax.ShapeDtypeStruct((m, BRANCH_W), F32),
                   jax.ShapeDtypeStruct((m, BRANCH_W), F32)],
        scratch_shapes=[pltpu.VMEM((m, 2 * HD_SB), BF16), pltpu.VMEM((m, HD_SB), BF16)],
        compiler_params=_params("arbitrary", "arbitrary"),
        name="sb_prompt",
    )(sb_bias, proj, proj, proj, proj)


def _sb_sample_kernel(pt_ref, bias_ref, q_ref, kn_ref, vn_ref, zg_ref, *rest, dec_seq, pages_per_step,
                      page_group):
    del pt_ref
    npg = pages_per_step
    k_refs, v_refs = rest[:npg], rest[npg:2 * npg]
    o_ref, acc_ref, carry_ref = rest[2 * npg:]
    step = pl.program_id(1)
    bk = PAGE_SIZE
    rows = H_SB * dec_seq
    suffix = _suffix_matrix(bk)
    bias = bias_ref[...]
    head_rows = lambda x, h: x[h * dec_seq:(h + 1) * dec_seq]

    def attend(zs, get_vs, masks, carry, acc):
        weights, carry = _sb_weights(zs, carry, suffix, masks)
        for a, get_v in zip(weights, get_vs):
            outs = [jnp.dot(head_rows(a, h).astype(BF16), get_v(h).astype(BF16), preferred_element_type=F32)
                    for h in range(H_SB)]
            acc = acc + jnp.concatenate(outs, axis=0)
        return carry, acc

    @pl.when(step == 0)
    def _():
        pad = jnp.zeros((bk - dec_seq, HD_SB), F32)
        zs = []
        for h in range(H_SB):
            sl = slice(h * HD_SB, (h + 1) * HD_SB)
            kh = jnp.concatenate([kn_ref[:, sl], pad], axis=0).astype(BF16)
            qh = (q_ref[:, sl] * SB_SCALE).astype(BF16)
            zs.append(lax.dot_general(qh, kh, (((1,), (1,)), ((), ())), preferred_element_type=F32))
        new_v = lambda h: jnp.concatenate([vn_ref[:, h * HD_SB:(h + 1) * HD_SB], pad], axis=0)
        t = lax.broadcasted_iota(jnp.int32, (rows, bk), 0) % dec_seq
        pos = lax.broadcasted_iota(jnp.int32, (rows, bk), 1)
        carry, acc = attend([jnp.concatenate(zs, axis=0) + bias], [new_v], [pos < t],
                            jnp.zeros((rows, 1), F32), jnp.zeros((rows, HD_SB), F32))
        carry_ref[...] = carry
        acc_ref[...] = acc

    rid = lax.broadcasted_iota(jnp.int32, (bk, HD_SB), 0) // dec_seq
    q_rows = []
    for h in range(H_SB):
        qh = jnp.concatenate([q_ref[:, h * HD_SB:(h + 1) * HD_SB] * SB_SCALE] * (bk // dec_seq), axis=0)
        q_rows.append(jnp.where(rid == h, qh, 0.0).astype(BF16))

    def logits(pages):
        zp = None
        for h in range(H_SB):
            kh = jnp.concatenate([k_refs[p][pl.ds(h, bk, stride=H_SB), :] for p in pages], axis=0)
            part = lax.dot_general(kh.astype(BF16), q_rows[h], (((1,), (1,)), ((), ())),
                                   preferred_element_type=F32)
            zp = part if zp is None else zp + part
        return [zp[i * bk:(i + 1) * bk].T[:rows] + bias for i in range(len(pages))]

    groups = [list(range(g, g + page_group)) for g in range(0, npg, page_group)]
    carry, acc = carry_ref[...], acc_ref[...]
    zs_next = logits(groups[0])
    for gi, pages in enumerate(groups):
        zs = zs_next
        if gi + 1 < len(groups):
            zs_next = logits(groups[gi + 1])
        get_vs = [lambda h, p=p: v_refs[p][pl.ds(h, bk, stride=H_SB), :] for p in pages]
        carry, acc = attend(zs, get_vs, [None] * len(pages), carry, acc)
    carry_ref[...] = carry
    acc_ref[...] = acc

    @pl.when(step == pl.num_programs(1) - 1)
    def _():
        for h in range(H_SB):
            sl = slice(h * HD_SB, (h + 1) * HD_SB)
            o_ref[:, sl] = head_rows(acc_ref, h) * _silu(zg_ref[:, sl])


def _sb_sample(proj, cache_k, cache_v, page_table, sb_bias, layer, dec_seq, pages_per_step=16, page_group=4):
    m = proj.shape[0]
    nb = m // dec_seq
    n_pages = page_table.shape[1]
    npg = pages_per_step
    assert n_pages % npg == 0 and PAGE_SIZE % dec_seq == 0
    cb = lambda col: col // BRANCH_W
    page = lambda p: (lambda b, s, pt: (layer, pt[b, n_pages - 1 - (s * npg + p)], 0, 0))
    row = lambda col: pl.BlockSpec((dec_seq, BRANCH_W), lambda b, s, pt, col=col: (b, cb(col)))
    rows = H_SB * dec_seq
    bias_tile = jnp.broadcast_to(jnp.repeat(sb_bias * LOG2E, dec_seq)[:, None], (rows, PAGE_SIZE))
    page_spec = lambda p: pl.BlockSpec((None, None, PAGE_SIZE * H_SB, HD_SB), page(p))
    grid_spec = pltpu.PrefetchScalarGridSpec(
        num_scalar_prefetch=1,
        grid=(nb, n_pages // npg),
        in_specs=[pl.BlockSpec((rows, PAGE_SIZE), lambda b, s, pt: (0, 0)),
                  row(COL_B_Q), row(COL_B_K), row(COL_B_V), row(COL_B_Z)]
                 + [page_spec(p) for p in range(npg)] + [page_spec(p) for p in range(npg)],
        out_specs=pl.BlockSpec((dec_seq, BRANCH_W), lambda b, s, pt: (b, 0)),
        scratch_shapes=[pltpu.VMEM((rows, HD_SB), F32), pltpu.VMEM((rows, 1), F32)],
    )
    return pl.pallas_call(
        functools.partial(_sb_sample_kernel, dec_seq=dec_seq, pages_per_step=npg, page_group=page_group),
        grid_spec=grid_spec,
        out_shape=jax.ShapeDtypeStruct((m, BRANCH_W), F32),
        compiler_params=_params("arbitrary", "arbitrary"),
        name="sb_sample",
    )(page_table, bias_tile, proj, proj, proj, proj, *([cache_k] * npg), *([cache_v] * npg))


def _softplus(x):
    return jnp.maximum(x, 0.0) + jnp.log(1.0 + jnp.exp(-jnp.abs(x)))


def _dot_exact_rhs(a_f32, sel_bf16):
    hi, mid, lo = _split3(a_f32)
    d = lambda p: jnp.dot(p, sel_bf16, preferred_element_type=F32)
    return d(hi) + (d(mid) + d(lo))


def _dot_exact_lhs(sel_bf16, b_f32):
    hi, mid, lo = _split3(b_f32)
    d = lambda p: jnp.dot(sel_bf16, p, preferred_element_type=F32)
    return d(hi) + (d(mid) + d(lo))


def _dot_x3(a, b):
    ah, al = _split2(a)
    bh, bl = _split2(b)
    d = lambda x, y: jnp.dot(x, y, preferred_element_type=F32)
    return d(ah, bh) + (d(ah, bl) + d(al, bh))


def _dn_prep_tail(seg_inputs, w_ref, ps_ref, alog_ref, dtb_ref, eg_ref, eb_ref,
                  qn_ref, kn_ref, vc_ref, g_ref, beta_ref):
    outs = (qn_ref, kn_ref, vc_ref)
    for seg, (x0, x1, x2, x3) in enumerate(seg_inputs):
        sl = slice(seg * BRANCH_W, (seg + 1) * BRANCH_W)
        w = w_ref[:, sl]
        y = _silu(((x3 * w[0:1] + x2 * w[1:2]) + x1 * w[2:3]) + x0 * w[3:4])
        if seg < 2:
            scale = DK_DN ** -0.5 if seg == 0 else 1.0
            for h in range(H_DN):
                hs = slice(h * DK_DN, (h + 1) * DK_DN)
                yh = y[:, hs]
                ss = jnp.sum(yh * yh, axis=-1, keepdims=True)
                outs[seg][:, hs] = yh * lax.rsqrt(ss + RMS_EPS) * scale
        else:
            outs[seg][...] = y
    ps = ps_ref[...]
    g = -jnp.exp(alog_ref[...]) * _softplus(ps + dtb_ref[...])
    g_ref[...] = _dot_exact_rhs(g, eg_ref[...])
    beta_ref[...] = _dot_exact_rhs(_sigmoid(ps), eb_ref[...])


def _dn_prep_prompt_kernel(xq_ref, xk_ref, xv_ref, hq_ref, hk_ref, hv_ref, *rest):
    first = pl.program_id(0) == 0
    segs = []
    for x_ref, h_ref in ((xq_ref, hq_ref), (xk_ref, hk_ref), (xv_ref, hv_ref)):
        x = x_ref[...]
        halo = jnp.where(first, 0.0, h_ref[...])
        rid = lax.broadcasted_iota(jnp.int32, halo.shape, 0)
        shifted = []
        for k in (1, 2, 3):
            xr = pltpu.roll(x, k, 0)
            head = jnp.where(rid < k, pltpu.roll(halo, k, 0), xr[:8])
            shifted.append(jnp.concatenate([head, xr[8:]], axis=0))
        segs.append((x, *shifted))
    _dn_prep_tail(segs, *rest)


def _dn_prep_sample_kernel(xq_ref, xk_ref, xv_ref, *rest):
    shifted = rest[:9]
    segs = []
    for s, x_ref in enumerate((xq_ref, xk_ref, xv_ref)):
        segs.append((x_ref[...], shifted[s][...], shifted[3 + s][...], shifted[6 + s][...]))
    _dn_prep_tail(segs, *rest[9:])


def _dn_prep(proj, proj_small, shifted, conv_w, a_log, dt_bias, tm=256):
    m = proj.shape[0]
    cb = lambda col: col // BRANCH_W
    xblk = lambda s: pl.BlockSpec((tm, BRANCH_W), lambda i, s=s: (i, cb(COL_C_QKV) + s))
    in_specs = [xblk(0), xblk(1), xblk(2)]
    args = [proj, proj, proj]
    if shifted is None:
        hblk = lambda s: pl.BlockSpec(
            (8, BRANCH_W), lambda i, s=s: (jnp.maximum(i * (tm // 8) - 1, 0), cb(COL_C_QKV) + s))
        in_specs += [hblk(0), hblk(1), hblk(2)]
        args += [proj, proj, proj]
        body = _dn_prep_prompt_kernel
    else:
        for arr in shifted:
            for s in range(3):
                in_specs.append(pl.BlockSpec((tm, BRANCH_W), lambda i, s=s: (i, s)))
                args.append(arr)
        body = _dn_prep_sample_kernel
    lane = lax.broadcasted_iota(jnp.int32, (LANES, BRANCH_W), 0)
    head = lax.broadcasted_iota(jnp.int32, (LANES, BRANCH_W), 1) // DV_DN
    e_g = (lane == head).astype(BF16)
    e_b = (lane == head + H_DN).astype(BF16)
    pad = lambda v, off: jnp.zeros((1, LANES), F32).at[0, off:off + H_DN].set(v)
    const = lambda shape: pl.BlockSpec(shape, lambda i: (0, 0))
    in_specs += [const((CONV_W, D_DN_CONV)), pl.BlockSpec((tm, LANES), lambda i: (i, COL_G_SMALL // LANES)),
                 const((1, LANES)), const((1, LANES)), const((LANES, BRANCH_W)), const((LANES, BRANCH_W))]
    args += [conv_w, proj_small, pad(a_log, 0), pad(dt_bias, 0), e_g, e_b]
    out = pl.BlockSpec((tm, BRANCH_W), lambda i: (i, 0))
    return pl.pallas_call(
        body,
        grid=(m // tm,),
        in_specs=in_specs,
        out_specs=[out] * 5,
        out_shape=[jax.ShapeDtypeStruct((m, BRANCH_W), F32)] * 5,
        compiler_params=_params("arbitrary"),
        name="dn_prep",
    )(*args)


def _dn_masks(c, block):
    i = lax.broadcasted_iota(jnp.int32, (c, c), 0)
    j = lax.broadcasted_iota(jnp.int32, (c, c), 1)
    same = (i // block) == (j // block)
    return dict(incl=same & (i >= j), strict=same & (i > j), incl_t=same & (i <= j), same=same,
                eye=(i == j))


def _dn_chunk_core(qs, ks, vs, g_reps, beta_reps, masks, n_double):
    c = qs[0].shape[0]
    each = lambda f, *lists: [f(*args) for args in zip(*lists)]
    onef = lambda m: jnp.where(m, 1.0, 0.0).astype(BF16)
    incl = masks["incl"]
    incl_b, same_b, ones_b = onef(incl), onef(masks["same"]), jnp.ones((c, c), BF16)
    contract_last = lambda x, y: lax.dot_general(x, y, (((1,), (1,)), ((), ())), preferred_element_type=F32)
    gcums = each(lambda g: _dot_exact_lhs(incl_b, g), g_reps)
    glasts = each(lambda g: _dot_exact_lhs(same_b, g), g_reps)
    g_rows = each(lambda g: _dot_exact_lhs(ones_b, jnp.where(masks["incl_t"], g[:, :c], 0.0)), g_reps)
    decays = each(lambda gc, gr: jnp.where(incl, jnp.exp(jnp.where(incl, gc[:, :c] - gr, 0.0)), 0.0),
                  gcums, g_rows)
    kbs = each(lambda k: k.astype(BF16), ks)
    kks = each(lambda kb: contract_last(kb, kb), kbs)
    qks = each(lambda q, kb, d: contract_last(q.astype(BF16), kb) * d, qs, kbs, decays)
    eye = jnp.where(masks["eye"], 1.0, 0.0)
    ps = each(lambda b, kk, d: jnp.where(masks["strict"], -(b[:, :c] * kk * d), 0.0), beta_reps, kks, decays)
    xs = each(lambda p: eye + p, ps)
    for _ in range(n_double):
        ps = each(lambda p: _dot_x3(p, p), ps)
        xs = each(lambda x, p: x + _dot_x3(x, p), xs, ps)
    gams = each(jnp.exp, gcums)
    u_parts = each(lambda x, b, v: _dot_x3(x, b * v), xs, beta_reps, vs)
    w_parts = each(lambda x, b, gam, k: _dot_x3(x, (b * gam) * k), xs, beta_reps, gams, ks)
    q_gs = each(lambda q, gam: q * gam, qs, gams)
    k_ends = each(lambda k, gl, gc: k * jnp.exp(gl - gc), ks, glasts, gcums)
    g_ends = each(jnp.exp, glasts)
    return u_parts, w_parts, qks, q_gs, k_ends, g_ends


def _gated_rms(o, norm_g, z):
    ms = jnp.mean(o * o, axis=-1, keepdims=True)
    return o * lax.rsqrt(ms + RMS_EPS) * norm_g * _silu(z)


def _dn_prompt_kernel(q_ref, k_ref, v_ref, g_ref, b_ref, z_ref, ng_ref, o_ref, sfin_ref, s_ref):
    n = pl.program_id(0)

    @pl.when(n == 0)
    def _():
        s_ref[...] = jnp.zeros_like(s_ref)

    c = q_ref.shape[0]
    masks = _dn_masks(c, c)
    heads = range(H_DN)
    cols = [slice(h * DV_DN, (h + 1) * DV_DN) for h in heads]
    take = lambda ref: [ref[:, hs] for hs in cols]
    u_ps, w_ps, qks, q_gs, k_ends, g_ends = _dn_chunk_core(
        take(q_ref), take(k_ref), take(v_ref), take(g_ref), take(b_ref), masks, int(math.log2(c)) - 1)
    dot = lambda x, y: jnp.dot(x, y, preferred_element_type=F32)
    ss = [s_ref[h] for h in heads]
    sbs = [s.astype(BF16) for s in ss]
    ubs = [(u_ps[h] - dot(w_ps[h].astype(BF16), sbs[h])).astype(BF16) for h in heads]
    os_ = [dot(q_gs[h].astype(BF16), sbs[h]) + dot(qks[h].astype(BF16), ubs[h]) for h in heads]
    for h in heads:
        s_ref[h] = g_ends[h][0:1, :] * ss[h] + lax.dot_general(
            k_ends[h].astype(BF16), ubs[h], (((0,), (0,)), ((), ())), preferred_element_type=F32)
        o_ref[:, cols[h]] = _gated_rms(os_[h], ng_ref[...], z_ref[:, cols[h]]).astype(o_ref.dtype)

    @pl.when(n == pl.num_programs(0) - 1)
    def _():
        sfin_ref[...] = s_ref[...]


def _dn_prompt(qn, kn, vc, g_rep, beta_rep, proj, norm_g):
    m = qn.shape[0]
    c = DN_CHUNK
    blk = pl.BlockSpec((c, BRANCH_W), lambda n: (n, 0))
    zblk = pl.BlockSpec((c, BRANCH_W), lambda n: (n, COL_C_Z // BRANCH_W))
    state = pl.BlockSpec((H_DN, DK_DN, DV_DN), lambda n: (0, 0, 0))
    return pl.pallas_call(
        _dn_prompt_kernel,
        grid=(m // c,),
        in_specs=[blk, blk, blk, blk, blk, zblk, pl.BlockSpec((1, DV_DN), lambda n: (0, 0))],
        out_specs=[blk, state],
        out_shape=[jax.ShapeDtypeStruct((m, BRANCH_W), BF16),
                   jax.ShapeDtypeStruct((H_DN, DK_DN, DV_DN), F32)],
        scratch_shapes=[pltpu.VMEM((H_DN, DK_DN, DV_DN), F32)],
        compiler_params=_params("arbitrary"),
        name="dn_prompt",
    )(qn, kn, vc, g_rep, beta_rep, proj, norm_g.reshape(1, DV_DN))


def _dn_sample_kernel(q_ref, k_ref, v_ref, g_ref, b_ref, z_ref, ng_ref, s0_ref, o_ref, snew_ref, *, dec_seq, n_seq):
    t = dec_seq
    rows = H_DN * t
    seqs, heads = range(n_seq), range(H_DN)
    cols = [slice(h * DV_DN, (h + 1) * DV_DN) for h in heads]
    srows = [slice(i * t, (i + 1) * t) for i in seqs]
    take = lambda ref: [jnp.concatenate([ref[srows[i], hs] for hs in cols], axis=0) for i in seqs]
    head_rows = lambda x, h: x[h * t:(h + 1) * t]
    masks = _dn_masks(rows, t)
    u_ps, w_ps, qks, q_gs, k_ends, g_ends = _dn_chunk_core(
        take(q_ref), take(k_ref), take(v_ref), take(g_ref), take(b_ref), masks, max(int(math.log2(t)) - 1, 0))
    dot = lambda x, y: jnp.dot(x, y, preferred_element_type=F32)
    sbs = [[s0_ref[i, h].astype(BF16) for h in heads] for i in seqs]
    ubs = [jnp.concatenate([head_rows(u_ps[i], h) - dot(head_rows(w_ps[i], h).astype(BF16), sbs[i][h])
                            for h in heads], axis=0).astype(BF16) for i in seqs]
    os_ = [jnp.concatenate([dot(head_rows(q_gs[i], h).astype(BF16), sbs[i][h]) for h in heads], axis=0)
           + dot(qks[i].astype(BF16), ubs[i]) for i in seqs]
    rid = lax.broadcasted_iota(jnp.int32, (rows, DK_DN), 0) // t
    for i in seqs:
        for h in heads:
            kh = jnp.where(rid == h, k_ends[i], 0.0).astype(BF16)
            upd = lax.dot_general(kh, ubs[i], (((0,), (0,)), ((), ())), preferred_element_type=F32)
            snew_ref[i, h] = head_rows(g_ends[i], h)[0:1, :] * s0_ref[i, h] + upd
            o_ref[srows[i], cols[h]] = _gated_rms(head_rows(os_[i], h), ng_ref[...],
                                                  z_ref[srows[i], cols[h]]).astype(o_ref.dtype)


def _dn_sample(qn, kn, vc, g_rep, beta_rep, proj, norm_g, s0, layer, dec_seq, n_seq=4):
    m = qn.shape[0]
    nb = m // dec_seq
    assert nb % n_seq == 0
    blk = pl.BlockSpec((n_seq * dec_seq, BRANCH_W), lambda b: (b, 0))
    zblk = pl.BlockSpec((n_seq * dec_seq, BRANCH_W), lambda b: (b, COL_C_Z // BRANCH_W))
    return pl.pallas_call(
        functools.partial(_dn_sample_kernel, dec_seq=dec_seq, n_seq=n_seq),
        grid=(nb // n_seq,),
        in_specs=[blk, blk, blk, blk, blk, zblk, pl.BlockSpec((1, DV_DN), lambda b: (0, 0)),
                  pl.BlockSpec((None, n_seq, H_DN, DK_DN, DV_DN), lambda b: (layer, b, 0, 0, 0))],
        out_specs=[blk, pl.BlockSpec((n_seq, H_DN, DK_DN, DV_DN), lambda b: (b, 0, 0, 0))],
        out_shape=[jax.ShapeDtypeStruct((m, BRANCH_W), BF16),
                   jax.ShapeDtypeStruct((nb, H_DN, DK_DN, DV_DN), F32)],
        compiler_params=_params("arbitrary"),
        name="dn_sample",
    )(qn, kn, vc, g_rep, beta_rep, proj, norm_g.reshape(1, DV_DN), s0)


def _mixer_layer(x, p, layer, sample, tm):
    m = x.shape[0]
    proj = _matmul(x, p["w_main"], layer, N_MAIN, tm, 1024, "proj_main")
    gates = _matmul(x, p["w_gate"], layer, N_GATE, tm, N_GATE // 7, "proj_gate")
    proj_small = gates

    if sample is None:
        w_mix, bias_full = p["sg_w"], p["sg_bias_prompt"]
    else:
        w_mix, bias_full = p["sg_w_sample"], p["sg_bias_sample"]
    o_a, sg_v = _spatial_gating(proj, w_mix, bias_full, p["sg_ln_g"], p["sg_ln_b"])

    if sample is None:
        o_b, k_rows, v_rows = _sb_prompt(proj, p["sb_bias"])
        qn, kn, vc, g_rep, beta_rep = _dn_prep(proj, proj_small, None, p["dn_conv_w"], p["dn_a_log"],
                                               p["dn_dt_bias"])
        o_c, s_new = _dn_prompt(qn, kn, vc, g_rep, beta_rep, proj, p["dn_norm_g"])
        conv_new = proj[m - (CONV_W - 1):, COL_C_QKV:COL_C_QKV + D_DN_CONV][None]
        s_new = s_new[None]
    else:
        t = sample["dec_seq"]
        nb = m // t
        o_b = _sb_sample(proj, sample["cache_k"], sample["cache_v"], sample["page_table"], p["sb_bias"],
                         layer, t)
        k_rows = proj[:, COL_B_K:COL_B_K + BRANCH_W]
        v_rows = proj[:, COL_B_V:COL_B_V + BRANCH_W]
        conv_in = proj[:, COL_C_QKV:COL_C_QKV + D_DN_CONV]
        xp = jnp.concatenate([sample["conv"], conv_in.reshape(nb, t, D_DN_CONV)], axis=1)
        shifted = [xp[:, CONV_W - 1 - k:CONV_W - 1 - k + t].reshape(m, D_DN_CONV) for k in (1, 2, 3)]
        qn, kn, vc, g_rep, beta_rep = _dn_prep(proj, proj_small, shifted, p["dn_conv_w"], p["dn_a_log"],
                                               p["dn_dt_bias"])
        o_c, s_new = _dn_sample(qn, kn, vc, g_rep, beta_rep, proj, p["dn_norm_g"], sample["ssm"], layer, t)
        conv_new = xp[:, t:]

    x_new = _merge_out(o_a, o_b, o_c, gates, p["w_branch"], p["gate_b"], p["w_out"], x, p["ln_g"], p["ln_b"])
    return x_new, k_rows, v_rows, conv_new, s_new, sg_v


def kernel(x_prompt, x_sample, cache_sb_k, cache_sb_v, state_dn_conv, state_dn_ssm, page_table, ln_in_g, ln_in_b,
           w_in, sg_ln_g, sg_ln_b, sg_w, sg_b, sb_bias, dn_conv_w, dn_a_log, dn_dt_bias, dn_norm_g, gate_b,
           w_branch, w_out, ln_g, ln_b):
    bp, seq, d = x_prompt.shape
    bs, dec_seq, _ = x_sample.shape
    assert bp == 1, "prompt rows are treated as one sequence"
    depth = w_in.shape[0]
    n_pool = cache_sb_k.shape[1]

    xp = _layer_norm(x_prompt.reshape(bp * seq, d), ln_in_g, ln_in_b)
    xs = _layer_norm(x_sample.reshape(bs * dec_seq, d), ln_in_g, ln_in_b)
    cache_k = cache_sb_k.reshape(depth, n_pool, PAGE_SIZE * H_SB, HD_SB)
    cache_v = cache_sb_v.reshape(depth, n_pool, PAGE_SIZE * H_SB, HD_SB)

    w_in_t = jnp.swapaxes(w_in, 1, 2)
    w_main = w_in_t
    w_gate = jnp.concatenate([w_in_t[:, COL_GATE:], w_in_t[:, COL_SMALL:COL_GATE],
                              jnp.zeros((depth, LANES - 2 * H_DN, d), F32)], axis=1)
    w_branch_b = w_branch.astype(BF16)
    w_out_b = w_out.astype(BF16)

    reps = SG_CHUNK // dec_seq
    eye = jnp.eye(reps, dtype=F32)
    outs = [[] for _ in range(9)]
    for l in range(depth):
        sg_small = jnp.tril(sg_w[l][:, :dec_seq, :dec_seq])
        p = {
            "w_main": w_main, "w_gate": w_gate,
            "sg_w": sg_w[l],
            "sg_bias_prompt": jnp.repeat(sg_b[l].T, BRANCH_W // SG_GROUPS, axis=1),
            "sg_w_sample": jax.vmap(lambda w: jnp.kron(eye, w))(sg_small),
            "sg_bias_sample": jnp.repeat(jnp.tile(sg_b[l][:, :dec_seq].T, (reps, 1)),
                                         BRANCH_W // SG_GROUPS, axis=1),
            "sg_ln_g": sg_ln_g[l], "sg_ln_b": sg_ln_b[l], "sb_bias": sb_bias[l],
            "dn_conv_w": dn_conv_w[l], "dn_a_log": dn_a_log[l], "dn_dt_bias": dn_dt_bias[l],
            "dn_norm_g": dn_norm_g[l], "gate_b": gate_b[l],
            "w_branch": w_branch_b[l], "w_out": w_out_b[l],
            "ln_g": ln_g[l], "ln_b": ln_b[l],
        }
        xp, kp, vp, cp, sp, _ = _mixer_layer(xp, p, l, None, tm=1024)
        sample = {"cache_k": cache_k, "cache_v": cache_v, "page_table": page_table,
                  "conv": state_dn_conv[l], "ssm": state_dn_ssm, "dec_seq": dec_seq}
        xs, ks, vs, cs, ss, sgv = _mixer_layer(xs, p, l, sample, tm=1024)
        hd = (H_SB, HD_SB)
        for lst, val in zip(outs, (kp.reshape(bp, seq, *hd), vp.reshape(bp, seq, *hd),
                                   ks.reshape(bs, dec_seq, *hd), vs.reshape(bs, dec_seq, *hd),
                                   cp, cs, sp, ss, sgv.reshape(bs, dec_seq, BRANCH_W))):
            lst.append(val)
    return (xp.reshape(bp, seq, d), xs.reshape(bs, dec_seq, d), *[jnp.stack(o) for o in outs])
```

```python
import functools
import math

import jax
import jax.numpy as jnp
from jax import lax
from jax.experimental import pallas as pl
from jax.experimental.pallas import tpu as pltpu

F32 = jnp.float32
BF16 = jnp.bfloat16

VMEM_LIMIT_BYTES = 56 * 1024 * 1024
LANES = 128

D_MODEL = 2048
BRANCH_W = 1024
N_BRANCH = 3
SG_GROUPS = 8
SG_CHUNK = 128
H_SB = 8
HD_SB = 128
H_DN = 8
DK_DN = 128
DV_DN = 128
D_DN_CONV = 3 * BRANCH_W
CONV_W = 4
DN_CHUNK = 64
PAGE_SIZE = 128
LN_EPS = 1e-5
RMS_EPS = 1e-6
DEPTH = 2
ALPHA = (2 * DEPTH) ** 0.25

COL_A_U, COL_A_V, COL_A_Z = 0, 1024, 2048
COL_B_Q, COL_B_K, COL_B_V, COL_B_Z = 3072, 4096, 5120, 6144
COL_C_QKV, COL_C_Z = 7168, 10240
N_MAIN = 11264
COL_SMALL = 11264
COL_GATE = COL_SMALL + 2 * H_DN
LOG2E = 1.4426950408889634


def _params(*sem):
    return pltpu.CompilerParams(dimension_semantics=sem, vmem_limit_bytes=VMEM_LIMIT_BYTES)


def _silu(x):
    return x * (1.0 / (1.0 + jnp.exp(-x)))


def _sigmoid(x):
    return 1.0 / (1.0 + jnp.exp(-x))


def _gelu(x):
    c = math.sqrt(2.0 / math.pi)
    return 0.5 * x * (1.0 + jnp.tanh(c * (x + 0.044715 * (x * x * x))))


def _layer_norm_rows(x, g, b):
    mu = jnp.mean(x, axis=-1, keepdims=True)
    xc = x - mu
    var = jnp.mean(xc * xc, axis=-1, keepdims=True)
    return xc * lax.rsqrt(var + LN_EPS) * g + b


def _ln_kernel(x_ref, g_ref, b_ref, o_ref):
    o_ref[...] = _layer_norm_rows(x_ref[...], g_ref[...], b_ref[...])


def _layer_norm(x, g, b, tm=512):
    m, d = x.shape
    tm = min(tm, m)
    return pl.pallas_call(
        _ln_kernel,
        grid=(m // tm,),
        in_specs=[pl.BlockSpec((tm, d), lambda i: (i, 0)),
                  pl.BlockSpec((1, d), lambda i: (0, 0)),
                  pl.BlockSpec((1, d), lambda i: (0, 0))],
        out_specs=pl.BlockSpec((tm, d), lambda i: (i, 0)),
        out_shape=jax.ShapeDtypeStruct((m, d), F32),
        compiler_params=_params("arbitrary"),
        name="ln_in",
    )(x, g.reshape(1, d), b.reshape(1, d))


def _mm_kernel(x_ref, w_ref, o_ref, xb_ref):
    @pl.when(pl.program_id(1) == 0)
    def _():
        xb_ref[...] = x_ref[...].astype(BF16)

    o_ref[...] = lax.dot_general(xb_ref[...], w_ref[...].astype(BF16), (((1,), (1,)), ((), ())),
                                 preferred_element_type=F32)


def _matmul(x, w_t, layer, n, tm, tn, name):
    m, k = x.shape
    tm = min(tm, m)
    return pl.pallas_call(
        _mm_kernel,
        grid=(m // tm, n // tn),
        in_specs=[pl.BlockSpec((tm, k), lambda i, j: (i, 0)),
                  pl.BlockSpec((None, tn, k), lambda i, j: (layer, j, 0))],
        out_specs=pl.BlockSpec((tm, tn), lambda i, j: (i, j)),
        out_shape=jax.ShapeDtypeStruct((m, n), F32),
        scratch_shapes=[pltpu.VMEM((tm, k), BF16)],
        compiler_params=_params("arbitrary", "arbitrary"),
        name=name,
    )(x, w_t)


def _sg_kernel(u_ref, v_ref, z_ref, w_ref, bias_ref, g_ref, b_ref, o_ref, vn_ref):
    u = _gelu(u_ref[...])
    vn = _layer_norm_rows(_gelu(v_ref[...]), g_ref[...], b_ref[...])
    vn_ref[...] = vn
    vb = vn.astype(BF16)
    rows = vn.shape[0]
    r = lax.broadcasted_iota(jnp.int32, (rows, rows), 0)
    c = lax.broadcasted_iota(jnp.int32, (rows, rows), 1)
    lower = c <= r
    parts = []
    for grp in range(SG_GROUPS):
        w = jnp.where(lower, w_ref[grp], 0.0).astype(BF16)
        parts.append(jnp.dot(w, vb[:, grp * LANES:(grp + 1) * LANES], preferred_element_type=F32))
    mixed = jnp.concatenate(parts, axis=-1) + bias_ref[...]
    o_ref[...] = (u * mixed * _silu(z_ref[...])).astype(o_ref.dtype)


def _spatial_gating(proj, w_mix, bias_full, ln_g, ln_b):
    m = proj.shape[0]
    tm = SG_CHUNK
    blk = lambda col: pl.BlockSpec((tm, BRANCH_W), lambda i, col=col: (i, col // BRANCH_W))
    return pl.pallas_call(
        _sg_kernel,
        grid=(m // tm,),
        in_specs=[blk(COL_A_U), blk(COL_A_V), blk(COL_A_Z),
                  pl.BlockSpec((SG_GROUPS, tm, tm), lambda i: (0, 0, 0)),
                  pl.BlockSpec((tm, BRANCH_W), lambda i: (0, 0)),
                  pl.BlockSpec((1, BRANCH_W), lambda i: (0, 0)),
                  pl.BlockSpec((1, BRANCH_W), lambda i: (0, 0))],
        out_specs=[pl.BlockSpec((tm, BRANCH_W), lambda i: (i, 0)),
                   pl.BlockSpec((tm, BRANCH_W), lambda i: (i, 0))],
        out_shape=[jax.ShapeDtypeStruct((m, BRANCH_W), BF16),
                   jax.ShapeDtypeStruct((m, BRANCH_W), F32)],
        compiler_params=_params("arbitrary"),
        name="spatial_gating",
    )(proj, proj, proj, w_mix, bias_full, ln_g.reshape(1, -1), ln_b.reshape(1, -1))


def _merge_out_kernel(oa_ref, ob_ref, oc_ref, wb_ref, ga_ref, gb_ref, gc_ref, gbias_ref, wo_ref, x_ref,
                      g_ref, b_ref, o_ref):
    merged = None
    for n, (o_n, gate_ref) in enumerate(((oa_ref, ga_ref), (ob_ref, gb_ref), (oc_ref, gc_ref))):
        y = jnp.dot(o_n[...].astype(BF16), wb_ref[n], preferred_element_type=F32)
        t = _sigmoid(gate_ref[...] + gbias_ref[n:n + 1, :]) * y
        merged = t if merged is None else merged + t
    out = jnp.dot(merged.astype(BF16), wo_ref[...], preferred_element_type=F32)
    o_ref[...] = _layer_norm_rows(ALPHA * x_ref[...] + out, g_ref[...], b_ref[...])


def _merge_out(o_a, o_b, o_c, gates, w_branch, gate_b, w_out, x, ln_g, ln_b, tm=256):
    m, d = x.shape
    tm = min(tm, m)
    resident = lambda shape: pl.BlockSpec(shape, lambda i: (0,) * len(shape), pipeline_mode=pl.Buffered(1))
    oblk = pl.BlockSpec((tm, BRANCH_W), lambda i: (i, 0))
    gblk = lambda n: pl.BlockSpec((tm, d), lambda i, n=n: (i, n))
    row = pl.BlockSpec((tm, d), lambda i: (i, 0))
    vec = pl.BlockSpec((1, d), lambda i: (0, 0))
    return pl.pallas_call(
        _merge_out_kernel,
        grid=(m // tm,),
        in_specs=[oblk, oblk, oblk, resident((N_BRANCH, BRANCH_W, d)), gblk(0), gblk(1), gblk(2),
                  pl.BlockSpec((N_BRANCH, d), lambda i: (0, 0)), resident((d, d)), row, vec, vec],
        out_specs=row,
        out_shape=jax.ShapeDtypeStruct((m, d), F32),
        compiler_params=_params("arbitrary"),
        name="merge_out",
    )(o_a, o_b, o_c, w_branch, gates, gates, gates, gate_b, w_out, x, ln_g.reshape(1, d), ln_b.reshape(1, d))


SB_SCALE = HD_SB ** -0.5 * LOG2E


def _split2(x):
    hi = x.astype(BF16)
    lo = (x - hi.astype(F32)).astype(BF16)
    return hi, lo


def _split3(x):
    hi = x.astype(BF16)
    r = x - hi.astype(F32)
    mid = r.astype(BF16)
    lo = (r - mid.astype(F32)).astype(BF16)
    return hi, mid, lo


def _suffix_matrix(bk):
    j = lax.broadcasted_iota(jnp.int32, (2 * bk, bk), 0) % bk
    s = lax.broadcasted_iota(jnp.int32, (2 * bk, bk), 1)
    return jnp.where(j > s, 1.0, 0.0).astype(BF16)


def _sb_weights(zs, spent, suffix, masks):
    sign = jnp.uint32(0x80000000)
    log_betas, costs, pieces = [], [], []
    for z, mask in zip(zs, masks):
        neg_abs = lax.bitcast_convert_type(lax.bitcast_convert_type(z, jnp.uint32) | sign, F32)
        cost = jnp.maximum(z, 0.0) + jnp.log2(1.0 + jnp.exp2(neg_abs))
        log_betas.append(z - cost)
        if mask is not None:
            cost = jnp.where(mask, cost, 0.0)
        costs.append(cost)
        pieces.append(jnp.concatenate(_split2(cost), axis=1))
    withins = [jnp.dot(piece, suffix, preferred_element_type=F32) for piece in pieces]
    weights = []
    for log_beta, cost, within, mask in zip(log_betas, costs, withins, masks):
        a = jnp.exp2(log_beta - (spent + within))
        weights.append(a if mask is None else jnp.where(mask, a, 0.0))
        spent = spent + jnp.sum(cost, axis=-1, keepdims=True)
    return weights, spent


def _sb_prompt_kernel(bias_ref, q_ref, k_ref, v_ref, zg_ref, o_ref, ko_ref, vo_ref, kb_ref, vb_ref, *, bq, bk,
                      unroll):
    h = pl.program_id(0)
    qi = pl.program_id(1)
    nsub = bq // bk

    @pl.when(qi == 0)
    def _():
        k = k_ref[...]
        v = v_ref[...]
        ko_ref[...] = k
        vo_ref[...] = v
        lane = lax.broadcasted_iota(jnp.int32, k.shape, 1)
        kb_ref[:, :HD_SB] = k.astype(BF16)
        kb_ref[:, HD_SB:] = jnp.where(lane < 3, 1.0, 0.0).astype(BF16)
        vb_ref[...] = v.astype(BF16)

    q = q_ref[...]
    lane = lax.broadcasted_iota(jnp.int32, q.shape, 1)
    b_hi, b_mid, b_lo = (piece.astype(F32) for piece in _split3(jnp.full(q.shape, bias_ref[h] * LOG2E, F32)))
    q_bias = jnp.where(lane == 0, b_hi, jnp.where(lane == 1, b_mid, jnp.where(lane == 2, b_lo, 0.0)))
    qb = jnp.concatenate([(q * SB_SCALE).astype(BF16), q_bias.astype(BF16)], axis=1)
    suffix = _suffix_matrix(bk)
    q_pos = qi * bq + lax.broadcasted_iota(jnp.int32, (bq, bk), 0)
    k_off = lax.broadcasted_iota(jnp.int32, (bq, bk), 1)

    def steps(js, carry, acc, masked):
        starts = [pl.multiple_of(j * bk, bk) for j in js]
        zs = [lax.dot_general(qb, kb_ref[pl.ds(st, bk), :], (((1,), (1,)), ((), ())),
                              preferred_element_type=F32) for st in starts]
        masks = [(j * bk + k_off) < q_pos if masked else None for j in js]
        weights, carry = _sb_weights(zs, carry, suffix, masks)
        for a, st in zip(weights, starts):
            acc = acc + jnp.dot(a.astype(BF16), vb_ref[pl.ds(st, bk), :], preferred_element_type=F32)
        return carry, acc

    carry = jnp.zeros((bq, 1), F32)
    acc = jnp.zeros((bq, HD_SB), F32)
    for t in range(nsub // unroll):
        first = qi * nsub + nsub - 1 - t * unroll
        carry, acc = steps([first - u for u in range(unroll)], carry, acc, True)

    def body(t, ca):
        first = qi * nsub - 1 - t * unroll
        return steps([first - u for u in range(unroll)], ca[0], ca[1], False)

    carry, acc = lax.fori_loop(0, qi * (nsub // unroll), body, (carry, acc))
    o_ref[...] = (acc * _silu(zg_ref[...])).astype(o_ref.dtype)


def _sb_prompt(proj, sb_bias, bq=1024, bk=256, unroll=2):
    m = proj.shape[0]
    assert (bq // bk) % unroll == 0
    cb = lambda col: col // HD_SB
    return pl.pallas_call(
        functools.partial(_sb_prompt_kernel, bq=bq, bk=bk, unroll=unroll),
        grid=(H_SB, m // bq),
        in_specs=[pl.BlockSpec(memory_space=pltpu.SMEM),
                  pl.BlockSpec((bq, HD_SB), lambda h, i: (i, cb(COL_B_Q) + h)),
                  pl.BlockSpec((m, HD_SB), lambda h, i: (0, cb(COL_B_K) + h)),
                  pl.BlockSpec((m, HD_SB), lambda h, i: (0, cb(COL_B_V) + h)),
                  pl.BlockSpec((bq, HD_SB), lambda h, i: (i, cb(COL_B_Z) + h))],
        out_specs=[pl.BlockSpec((bq, HD_SB), lambda h, i: (i, h)),
                   pl.BlockSpec((m, HD_SB), lambda h, i: (0, h)),
                   pl.BlockSpec((m, HD_SB), lambda h, i: (0, h))],
        out_shape=[jax.ShapeDtypeStruct((m, BRANCH_W), BF16),
                   jax.ShapeDtypeStruct((m, BRANCH_W), F32),
                   jax.ShapeDtypeStruct((m, BRANCH_W), F32)],
        scratch_shapes=[pltpu.VMEM((m, 2 * HD_SB), BF16), pltpu.VMEM((m, HD_SB), BF16)],
        compiler_params=_params("arbitrary", "arbitrary"),
        name="sb_prompt",
    )(sb_bias, proj, proj, proj, proj)


def _sb_sample_kernel(pt_ref, bias_ref, q_ref, kn_ref, vn_ref, zg_ref, *rest, dec_seq, pages_per_step,
                      page_group):
    del pt_ref
    npg = pages_per_step
    k_refs, v_refs = rest[:npg], rest[npg:2 * npg]
    o_ref, acc_ref, carry_ref = rest[2 * npg:]
    step = pl.program_id(1)
    bk = PAGE_SIZE
    rows = H_SB * dec_seq
    suffix = _suffix_matrix(bk)
    bias = bias_ref[...]
    head_rows = lambda x, h: x[h * dec_seq:(h + 1) * dec_seq]

    def attend(zs, get_vs, masks, carry, acc):
        weights, carry = _sb_weights(zs, carry, suffix, masks)
        for a, get_v in zip(weights, get_vs):
            outs = [jnp.dot(head_rows(a, h).astype(BF16), get_v(h).astype(BF16), preferred_element_type=F32)
                    for h in range(H_SB)]
            acc = acc + jnp.concatenate(outs, axis=0)
        return carry, acc

    @pl.when(step == 0)
    def _():
        pad = jnp.zeros((bk - dec_seq, HD_SB), F32)
        zs = []
        for h in range(H_SB):
            sl = slice(h * HD_SB, (h + 1) * HD_SB)
            kh = jnp.concatenate([kn_ref[:, sl], pad], axis=0).astype(BF16)
            qh = (q_ref[:, sl] * SB_SCALE).astype(BF16)
            zs.append(lax.dot_general(qh, kh, (((1,), (1,)), ((), ())), preferred_element_type=F32))
        new_v = lambda h: jnp.concatenate([vn_ref[:, h * HD_SB:(h + 1) * HD_SB], pad], axis=0)
        t = lax.broadcasted_iota(jnp.int32, (rows, bk), 0) % dec_seq
        pos = lax.broadcasted_iota(jnp.int32, (rows, bk), 1)
        carry, acc = attend([jnp.concatenate(zs, axis=0) + bias], [new_v], [pos < t],
                            jnp.zeros((rows, 1), F32), jnp.zeros((rows, HD_SB), F32))
        carry_ref[...] = carry
        acc_ref[...] = acc

    rid = lax.broadcasted_iota(jnp.int32, (bk, HD_SB), 0) // dec_seq
    q_rows = []
    for h in range(H_SB):
        qh = jnp.concatenate([q_ref[:, h * HD_SB:(h + 1) * HD_SB] * SB_SCALE] * (bk // dec_seq), axis=0)
        q_rows.append(jnp.where(rid == h, qh, 0.0).astype(BF16))

    def logits(pages):
        zp = None
        for h in range(H_SB):
            kh = jnp.concatenate([k_refs[p][pl.ds(h, bk, stride=H_SB), :] for p in pages], axis=0)
            part = lax.dot_general(kh.astype(BF16), q_rows[h], (((1,), (1,)), ((), ())),
                                   preferred_element_type=F32)
            zp = part if zp is None else zp + part
        return [zp[i * bk:(i + 1) * bk].T[:rows] + bias for i in range(len(pages))]

    groups = [list(range(g, g + page_group)) for g in range(0, npg, page_group)]
    carry, acc = carry_ref[...], acc_ref[...]
    zs_next = logits(groups[0])
    for gi, pages in enumerate(groups):
        zs = zs_next
        if gi + 1 < len(groups):
            zs_next = logits(groups[gi + 1])
        get_vs = [lambda h, p=p: v_refs[p][pl.ds(h, bk, stride=H_SB), :] for p in pages]
        carry, acc = attend(zs, get_vs, [None] * len(pages), carry, acc)
    carry_ref[...] = carry
    acc_ref[...] = acc

    @pl.when(step == pl.num_programs(1) - 1)
    def _():
        for h in range(H_SB):
            sl = slice(h * HD_SB, (h + 1) * HD_SB)
            o_ref[:, sl] = head_rows(acc_ref, h) * _silu(zg_ref[:, sl])


def _sb_sample(proj, cache_k, cache_v, page_table, sb_bias, layer, dec_seq, pages_per_step=16, page_group=4):
    m = proj.shape[0]
    nb = m // dec_seq
    n_pages = page_table.shape[1]
    npg = pages_per_step
    assert n_pages % npg == 0 and PAGE_SIZE % dec_seq == 0
    cb = lambda col: col // BRANCH_W
    page = lambda p: (lambda b, s, pt: (layer, pt[b, n_pages - 1 - (s * npg + p)], 0, 0))
    row = lambda col: pl.BlockSpec((dec_seq, BRANCH_W), lambda b, s, pt, col=col: (b, cb(col)))
    rows = H_SB * dec_seq
    bias_tile = jnp.broadcast_to(jnp.repeat(sb_bias * LOG2E, dec_seq)[:, None], (rows, PAGE_SIZE))
    page_spec = lambda p: pl.BlockSpec((None, None, PAGE_SIZE * H_SB, HD_SB), page(p))
    grid_spec = pltpu.PrefetchScalarGridSpec(
        num_scalar_prefetch=1,
        grid=(nb, n_pages // npg),
        in_specs=[pl.BlockSpec((rows, PAGE_SIZE), lambda b, s, pt: (0, 0)),
                  row(COL_B_Q), row(COL_B_K), row(COL_B_V), row(COL_B_Z)]
                 + [page_spec(p) for p in range(npg)] + [page_spec(p) for p in range(npg)],
        out_specs=pl.BlockSpec((dec_seq, BRANCH_W), lambda b, s, pt: (b, 0)),
        scratch_shapes=[pltpu.VMEM((rows, HD_SB), F32), pltpu.VMEM((rows, 1), F32)],
    )
    return pl.pallas_call(
        functools.partial(_sb_sample_kernel, dec_seq=dec_seq, pages_per_step=npg, page_group=page_group),
        grid_spec=grid_spec,
        out_shape=jax.ShapeDtypeStruct((m, BRANCH_W), F32),
        compiler_params=_params("arbitrary", "arbitrary"),
        name="sb_sample",
    )(page_table, bias_tile, proj, proj, proj, proj, *([cache_k] * npg), *([cache_v] * npg))


def _softplus(x):
    return jnp.maximum(x, 0.0) + jnp.log(1.0 + jnp.exp(-jnp.abs(x)))


def _dot_exact_rhs(a_f32, sel_bf16):
    hi, mid, lo = _split3(a_f32)
    d = lambda p: jnp.dot(p, sel_bf16, preferred_element_type=F32)
    return d(hi) + (d(mid) + d(lo))


def _dot_exact_lhs(sel_bf16, b_f32):
    hi, mid, lo = _split3(b_f32)
    d = lambda p: jnp.dot(sel_bf16, p, preferred_element_type=F32)
    return d(hi) + (d(mid) + d(lo))


def _dot_x3(a, b):
    ah, al = _split2(a)
    bh, bl = _split2(b)
    d = lambda x, y: jnp.dot(x, y, preferred_element_type=F32)
    return d(ah, bh) + (d(ah, bl) + d(al, bh))


def _dn_prep_tail(seg_inputs, w_ref, x_ref, ws_ref, alog_ref, dtb_ref, eg_ref, eb_ref,
                  qn_ref, kn_ref, vc_ref, g_ref, beta_ref):
    outs = (qn_ref, kn_ref, vc_ref)
    for seg, (x0, x1, x2, x3) in enumerate(seg_inputs):
        sl = slice(seg * BRANCH_W, (seg + 1) * BRANCH_W)
        w = w_ref[:, sl]
        y = _silu(((x3 * w[0:1] + x2 * w[1:2]) + x1 * w[2:3]) + x0 * w[3:4])
        if seg < 2:
            scale = DK_DN ** -0.5 if seg == 0 else 1.0
            for h in range(H_DN):
                hs = slice(h * DK_DN, (h + 1) * DK_DN)
                yh = y[:, hs]
                ss = jnp.sum(yh * yh, axis=-1, keepdims=True)
                outs[seg][:, hs] = yh * lax.rsqrt(ss + RMS_EPS) * scale
        else:
            outs[seg][...] = y
    ps = lax.dot_general(x_ref[...].astype(BF16), ws_ref[...].astype(BF16), (((1,), (1,)), ((), ())),
                         preferred_element_type=F32)
    g = -jnp.exp(alog_ref[...]) * _softplus(ps + dtb_ref[...])
    g_ref[...] = _dot_exact_rhs(g, eg_ref[...])
    beta_ref[...] = _dot_exact_rhs(_sigmoid(ps), eb_ref[...])


def _dn_prep_prompt_kernel(xq_ref, xk_ref, xv_ref, hq_ref, hk_ref, hv_ref, *rest):
    first = pl.program_id(0) == 0
    segs = []
    for x_ref, h_ref in ((xq_ref, hq_ref), (xk_ref, hk_ref), (xv_ref, hv_ref)):
        x = x_ref[...]
        halo = jnp.where(first, 0.0, h_ref[...])
        rid = lax.broadcasted_iota(jnp.int32, halo.shape, 0)
        shifted = []
        for k in (1, 2, 3):
            xr = pltpu.roll(x, k, 0)
            head = jnp.where(rid < k, pltpu.roll(halo, k, 0), xr[:8])
            shifted.append(jnp.concatenate([head, xr[8:]], axis=0))
        segs.append((x, *shifted))
    _dn_prep_tail(segs, *rest)


def _dn_prep_sample_kernel(xq_ref, xk_ref, xv_ref, *rest):
    shifted = rest[:9]
    segs = []
    for s, x_ref in enumerate((xq_ref, xk_ref, xv_ref)):
        segs.append((x_ref[...], shifted[s][...], shifted[3 + s][...], shifted[6 + s][...]))
    _dn_prep_tail(segs, *rest[9:])


def _dn_prep(proj, x, w_small_t, layer, shifted, conv_w, a_log, dt_bias, tm=256):
    m = proj.shape[0]
    cb = lambda col: col // BRANCH_W
    xblk = lambda s: pl.BlockSpec((tm, BRANCH_W), lambda i, s=s: (i, cb(COL_C_QKV) + s))
    in_specs = [xblk(0), xblk(1), xblk(2)]
    args = [proj, proj, proj]
    if shifted is None:
        hblk = lambda s: pl.BlockSpec(
            (8, BRANCH_W), lambda i, s=s: (jnp.maximum(i * (tm // 8) - 1, 0), cb(COL_C_QKV) + s))
        in_specs += [hblk(0), hblk(1), hblk(2)]
        args += [proj, proj, proj]
        body = _dn_prep_prompt_kernel
    else:
        for arr in shifted:
            for s in range(3):
                in_specs.append(pl.BlockSpec((tm, BRANCH_W), lambda i, s=s: (i, s)))
                args.append(arr)
        body = _dn_prep_sample_kernel
    lane = lax.broadcasted_iota(jnp.int32, (LANES, BRANCH_W), 0)
    head = lax.broadcasted_iota(jnp.int32, (LANES, BRANCH_W), 1) // DV_DN
    e_g = (lane == head).astype(BF16)
    e_b = (lane == head + H_DN).astype(BF16)
    pad = lambda v, off: jnp.zeros((1, LANES), F32).at[0, off:off + H_DN].set(v)
    const = lambda shape: pl.BlockSpec(shape, lambda i: (0, 0))
    in_specs += [const((CONV_W, D_DN_CONV)), pl.BlockSpec((tm, D_MODEL), lambda i: (i, 0)),
                 pl.BlockSpec((None, LANES, D_MODEL), lambda i: (layer, 0, 0)),
                 const((1, LANES)), const((1, LANES)), const((LANES, BRANCH_W)), const((LANES, BRANCH_W))]
    args += [conv_w, x, w_small_t, pad(a_log, 0), pad(dt_bias, 0), e_g, e_b]
    out = pl.BlockSpec((tm, BRANCH_W), lambda i: (i, 0))
    return pl.pallas_call(
        body,
        grid=(m // tm,),
        in_specs=in_specs,
        out_specs=[out] * 5,
        out_shape=[jax.ShapeDtypeStruct((m, BRANCH_W), F32)] * 5,
        compiler_params=_params("arbitrary"),
        name="dn_prep",
    )(*args)


def _dn_masks(c, block):
    i = lax.broadcasted_iota(jnp.int32, (c, c), 0)
    j = lax.broadcasted_iota(jnp.int32, (c, c), 1)
    same = (i // block) == (j // block)
    return dict(incl=same & (i >= j), strict=same & (i > j), incl_t=same & (i <= j), same=same,
                eye=(i == j))


def _dn_chunk_core(qs, ks, vs, g_reps, beta_reps, masks, n_double):
    c = qs[0].shape[0]
    each = lambda f, *lists: [f(*args) for args in zip(*lists)]
    onef = lambda m: jnp.where(m, 1.0, 0.0).astype(BF16)
    incl = masks["incl"]
    incl_b, same_b, ones_b = onef(incl), onef(masks["same"]), jnp.ones((c, c), BF16)
    contract_last = lambda x, y: lax.dot_general(x, y, (((1,), (1,)), ((), ())), preferred_element_type=F32)
    gcums = each(lambda g: _dot_exact_lhs(incl_b, g), g_reps)
    glasts = each(lambda g: _dot_exact_lhs(same_b, g), g_reps)
    g_rows = each(lambda g: _dot_exact_lhs(ones_b, jnp.where(masks["incl_t"], g[:, :c], 0.0)), g_reps)
    decays = each(lambda gc, gr: jnp.where(incl, jnp.exp(jnp.where(incl, gc[:, :c] - gr, 0.0)), 0.0),
                  gcums, g_rows)
    kbs = each(lambda k: k.astype(BF16), ks)
    kks = each(lambda kb: contract_last(kb, kb), kbs)
    qks = each(lambda q, kb, d: contract_last(q.astype(BF16), kb) * d, qs, kbs, decays)
    eye = jnp.where(masks["eye"], 1.0, 0.0)
    ps = each(lambda b, kk, d: jnp.where(masks["strict"], -(b[:, :c] * kk * d), 0.0), beta_reps, kks, decays)
    xs = each(lambda p: eye + p, ps)
    for _ in range(n_double):
        ps = each(lambda p: _dot_x3(p, p), ps)
        xs = each(lambda x, p: x + _dot_x3(x, p), xs, ps)
    gams = each(jnp.exp, gcums)
    u_parts = each(lambda x, b, v: _dot_x3(x, b * v), xs, beta_reps, vs)
    w_parts = each(lambda x, b, gam, k: _dot_x3(x, (b * gam) * k), xs, beta_reps, gams, ks)
    q_gs = each(lambda q, gam: q * gam, qs, gams)
    k_ends = each(lambda k, gl, gc: k * jnp.exp(gl - gc), ks, glasts, gcums)
    g_ends = each(jnp.exp, glasts)
    return u_parts, w_parts, qks, q_gs, k_ends, g_ends


def _gated_rms(o, norm_g, z):
    ms = jnp.mean(o * o, axis=-1, keepdims=True)
    return o * lax.rsqrt(ms + RMS_EPS) * norm_g * _silu(z)


def _dn_prompt_kernel(q_ref, k_ref, v_ref, g_ref, b_ref, z_ref, ng_ref, o_ref, sfin_ref, s_ref):
    n = pl.program_id(0)

    @pl.when(n == 0)
    def _():
        s_ref[...] = jnp.zeros_like(s_ref)

    c = q_ref.shape[0]
    masks = _dn_masks(c, c)
    heads = range(H_DN)
    cols = [slice(h * DV_DN, (h + 1) * DV_DN) for h in heads]
    take = lambda ref: [ref[:, hs] for hs in cols]
    u_ps, w_ps, qks, q_gs, k_ends, g_ends = _dn_chunk_core(
        take(q_ref), take(k_ref), take(v_ref), take(g_ref), take(b_ref), masks, int(math.log2(c)) - 1)
    dot = lambda x, y: jnp.dot(x, y, preferred_element_type=F32)
    ss = [s_ref[h] for h in heads]
    sbs = [s.astype(BF16) for s in ss]
    ubs = [(u_ps[h] - dot(w_ps[h].astype(BF16), sbs[h])).astype(BF16) for h in heads]
    os_ = [dot(q_gs[h].astype(BF16), sbs[h]) + dot(qks[h].astype(BF16), ubs[h]) for h in heads]
    for h in heads:
        s_ref[h] = g_ends[h][0:1, :] * ss[h] + lax.dot_general(
            k_ends[h].astype(BF16), ubs[h], (((0,), (0,)), ((), ())), preferred_element_type=F32)
        o_ref[:, cols[h]] = _gated_rms(os_[h], ng_ref[...], z_ref[:, cols[h]]).astype(o_ref.dtype)

    @pl.when(n == pl.num_programs(0) - 1)
    def _():
        sfin_ref[...] = s_ref[...]


def _dn_prompt(qn, kn, vc, g_rep, beta_rep, proj, norm_g):
    m = qn.shape[0]
    c = DN_CHUNK
    blk = pl.BlockSpec((c, BRANCH_W), lambda n: (n, 0))
    zblk = pl.BlockSpec((c, BRANCH_W), lambda n: (n, COL_C_Z // BRANCH_W))
    state = pl.BlockSpec((H_DN, DK_DN, DV_DN), lambda n: (0, 0, 0))
    return pl.pallas_call(
        _dn_prompt_kernel,
        grid=(m // c,),
        in_specs=[blk, blk, blk, blk, blk, zblk, pl.BlockSpec((1, DV_DN), lambda n: (0, 0))],
        out_specs=[blk, state],
        out_shape=[jax.ShapeDtypeStruct((m, BRANCH_W), BF16),
                   jax.ShapeDtypeStruct((H_DN, DK_DN, DV_DN), F32)],
        scratch_shapes=[pltpu.VMEM((H_DN, DK_DN, DV_DN), F32)],
        compiler_params=_params("arbitrary"),
        name="dn_prompt",
    )(qn, kn, vc, g_rep, beta_rep, proj, norm_g.reshape(1, DV_DN))


def _dn_sample_kernel(q_ref, k_ref, v_ref, g_ref, b_ref, z_ref, ng_ref, s0_ref, o_ref, snew_ref, *, dec_seq, n_seq):
    t = dec_seq
    rows = H_DN * t
    seqs, heads = range(n_seq), range(H_DN)
    cols = [slice(h * DV_DN, (h + 1) * DV_DN) for h in heads]
    srows = [slice(i * t, (i + 1) * t) for i in seqs]
    take = lambda ref: [jnp.concatenate([ref[srows[i], hs] for hs in cols], axis=0) for i in seqs]
    head_rows = lambda x, h: x[h * t:(h + 1) * t]
    masks = _dn_masks(rows, t)
    u_ps, w_ps, qks, q_gs, k_ends, g_ends = _dn_chunk_core(
        take(q_ref), take(k_ref), take(v_ref), take(g_ref), take(b_ref), masks, max(int(math.log2(t)) - 1, 0))
    dot = lambda x, y: jnp.dot(x, y, preferred_element_type=F32)
    sbs = [[s0_ref[i, h].astype(BF16) for h in heads] for i in seqs]
    ubs = [jnp.concatenate([head_rows(u_ps[i], h) - dot(head_rows(w_ps[i], h).astype(BF16), sbs[i][h])
                            for h in heads], axis=0).astype(BF16) for i in seqs]
    os_ = [jnp.concatenate([dot(head_rows(q_gs[i], h).astype(BF16), sbs[i][h]) for h in heads], axis=0)
           + dot(qks[i].astype(BF16), ubs[i]) for i in seqs]
    rid = lax.broadcasted_iota(jnp.int32, (rows, DK_DN), 0) // t
    for i in seqs:
        for h in heads:
            kh = jnp.where(rid == h, k_ends[i], 0.0).astype(BF16)
            upd = lax.dot_general(kh, ubs[i], (((0,), (0,)), ((), ())), preferred_element_type=F32)
            snew_ref[i, h] = head_rows(g_ends[i], h)[0:1, :] * s0_ref[i, h] + upd
            o_ref[srows[i], cols[h]] = _gated_rms(head_rows(os_[i], h), ng_ref[...],
                                                  z_ref[srows[i], cols[h]]).astype(o_ref.dtype)


def _dn_sample(qn, kn, vc, g_rep, beta_rep, proj, norm_g, s0, layer, dec_seq, n_seq=4):
    m = qn.shape[0]
    nb = m // dec_seq
    assert nb % n_seq == 0
    blk = pl.BlockSpec((n_seq * dec_seq, BRANCH_W), lambda b: (b, 0))
    zblk = pl.BlockSpec((n_seq * dec_seq, BRANCH_W), lambda b: (b, COL_C_Z // BRANCH_W))
    return pl.pallas_call(
        functools.partial(_dn_sample_kernel, dec_seq=dec_seq, n_seq=n_seq),
        grid=(nb // n_seq,),
        in_specs=[blk, blk, blk, blk, blk, zblk, pl.BlockSpec((1, DV_DN), lambda b: (0, 0)),
                  pl.BlockSpec((None, n_seq, H_DN, DK_DN, DV_DN), lambda b: (layer, b, 0, 0, 0))],
        out_specs=[blk, pl.BlockSpec((n_seq, H_DN, DK_DN, DV_DN), lambda b: (b, 0, 0, 0))],
        out_shape=[jax.ShapeDtypeStruct((m, BRANCH_W), BF16),
                   jax.ShapeDtypeStruct((nb, H_DN, DK_DN, DV_DN), F32)],
        compiler_params=_params("arbitrary"),
        name="dn_sample",
    )(qn, kn, vc, g_rep, beta_rep, proj, norm_g.reshape(1, DV_DN), s0)


def _mixer_layer(x, p, layer, sample, tm):
    m = x.shape[0]
    proj = _matmul(x, p["w_main"], layer, N_MAIN, tm, 1024, "proj_main")
    gates = _matmul(x, p["w_gate"], layer, N_BRANCH * D_MODEL, tm, 1024, "proj_gate")

    if sample is None:
        w_mix, bias_full = p["sg_w"], p["sg_bias_prompt"]
    else:
        w_mix, bias_full = p["sg_w_sample"], p["sg_bias_sample"]
    o_a, sg_v = _spatial_gating(proj, w_mix, bias_full, p["sg_ln_g"], p["sg_ln_b"])

    if sample is None:
        o_b, k_rows, v_rows = _sb_prompt(proj, p["sb_bias"])
        qn, kn, vc, g_rep, beta_rep = _dn_prep(proj, x, p["w_small"], layer, None, p["dn_conv_w"],
                                               p["dn_a_log"], p["dn_dt_bias"])
        o_c, s_new = _dn_prompt(qn, kn, vc, g_rep, beta_rep, proj, p["dn_norm_g"])
        conv_new = proj[m - (CONV_W - 1):, COL_C_QKV:COL_C_QKV + D_DN_CONV][None]
        s_new = s_new[None]
    else:
        t = sample["dec_seq"]
        nb = m // t
        o_b = _sb_sample(proj, sample["cache_k"], sample["cache_v"], sample["page_table"], p["sb_bias"],
                         layer, t)
        k_rows = proj[:, COL_B_K:COL_B_K + BRANCH_W]
        v_rows = proj[:, COL_B_V:COL_B_V + BRANCH_W]
        conv_in = proj[:, COL_C_QKV:COL_C_QKV + D_DN_CONV]
        xp = jnp.concatenate([sample["conv"], conv_in.reshape(nb, t, D_DN_CONV)], axis=1)
        shifted = [xp[:, CONV_W - 1 - k:CONV_W - 1 - k + t].reshape(m, D_DN_CONV) for k in (1, 2, 3)]
        qn, kn, vc, g_rep, beta_rep = _dn_prep(proj, x, p["w_small"], layer, shifted, p["dn_conv_w"],
                                               p["dn_a_log"], p["dn_dt_bias"])
        o_c, s_new = _dn_sample(qn, kn, vc, g_rep, beta_rep, proj, p["dn_norm_g"], sample["ssm"], layer, t)
        conv_new = xp[:, t:]

    x_new = _merge_out(o_a, o_b, o_c, gates, p["w_branch"], p["gate_b"], p["w_out"], x, p["ln_g"], p["ln_b"])
    return x_new, k_rows, v_rows, conv_new, s_new, sg_v


def kernel(x_prompt, x_sample, cache_sb_k, cache_sb_v, state_dn_conv, state_dn_ssm, page_table, ln_in_g, ln_in_b,
           w_in, sg_ln_g, sg_ln_b, sg_w, sg_b, sb_bias, dn_conv_w, dn_a_log, dn_dt_bias, dn_norm_g, gate_b,
           w_branch, w_out, ln_g, ln_b):
    bp, seq, d = x_prompt.shape
    bs, dec_seq, _ = x_sample.shape
    assert bp == 1, "prompt rows are treated as one sequence"
    depth = w_in.shape[0]
    n_pool = cache_sb_k.shape[1]

    xp = _layer_norm(x_prompt.reshape(bp * seq, d), ln_in_g, ln_in_b)
    xs = _layer_norm(x_sample.reshape(bs * dec_seq, d), ln_in_g, ln_in_b)
    cache_k = cache_sb_k.reshape(depth, n_pool, PAGE_SIZE * H_SB, HD_SB)
    cache_v = cache_sb_v.reshape(depth, n_pool, PAGE_SIZE * H_SB, HD_SB)

    w_in_t = jnp.swapaxes(w_in, 1, 2)
    w_main = w_in_t
    w_gate = w_in_t[:, COL_GATE:]
    w_small = jnp.pad(w_in_t[:, COL_SMALL:COL_GATE], ((0, 0), (0, LANES - 2 * H_DN), (0, 0)))
    w_branch_b = w_branch.astype(BF16)
    w_out_b = w_out.astype(BF16)

    reps = SG_CHUNK // dec_seq
    eye = jnp.eye(reps, dtype=F32)
    outs = [[] for _ in range(9)]
    for l in range(depth):
        sg_small = jnp.tril(sg_w[l][:, :dec_seq, :dec_seq])
        p = {
            "w_main": w_main, "w_gate": w_gate, "w_small": w_small,
            "sg_w": sg_w[l],
            "sg_bias_prompt": jnp.repeat(sg_b[l].T, BRANCH_W // SG_GROUPS, axis=1),
            "sg_w_sample": jax.vmap(lambda w: jnp.kron(eye, w))(sg_small),
            "sg_bias_sample": jnp.repeat(jnp.tile(sg_b[l][:, :dec_seq].T, (reps, 1)),
                                         BRANCH_W // SG_GROUPS, axis=1),
            "sg_ln_g": sg_ln_g[l], "sg_ln_b": sg_ln_b[l], "sb_bias": sb_bias[l],
            "dn_conv_w": dn_conv_w[l], "dn_a_log": dn_a_log[l], "dn_dt_bias": dn_dt_bias[l],
            "dn_norm_g": dn_norm_g[l], "gate_b": gate_b[l],
            "w_branch": w_branch_b[l], "w_out": w_out_b[l],
            "ln_g": ln_g[l], "ln_b": ln_b[l],
        }
        xp, kp, vp, cp, sp, _ = _mixer_layer(xp, p, l, None, tm=1024)
        sample = {"cache_k": cache_k, "cache_v": cache_v, "page_table": page_table,
                  "conv": state_dn_conv[l], "ssm": state_dn_ssm, "dec_seq": dec_seq}
        xs, ks, vs, cs, ss, sgv = _mixer_layer(xs, p, l, sample, tm=1024)
        hd = (H_SB, HD_SB)
        for lst, val in zip(outs, (kp.reshape(bp, seq, *hd), vp.reshape(bp, seq, *hd),
                                   ks.reshape(bs, dec_seq, *hd), vs.reshape(bs, dec_seq, *hd),
                                   cp, cs, sp, ss, sgv.reshape(bs, dec_seq, BRANCH_W))):
            lst.append(val)
    return (xp.reshape(bp, seq, d), xs.reshape(bs, dec_seq, d), *[jnp.stack(o) for o in outs])
```

```python
import functools
import math

import jax
import jax.numpy as jnp
from jax import lax
from jax.experimental import pallas as pl
from jax.experimental.pallas import tpu as pltpu

F32 = jnp.float32
BF16 = jnp.bfloat16

VMEM_LIMIT_BYTES = 56 * 1024 * 1024
LANES = 128

D_MODEL = 2048
BRANCH_W = 1024
N_BRANCH = 3
SG_GROUPS = 8
SG_CHUNK = 128
H_SB = 8
HD_SB = 128
H_DN = 8
DK_DN = 128
DV_DN = 128
D_DN_CONV = 3 * BRANCH_W
CONV_W = 4
DN_CHUNK = 64
PAGE_SIZE = 128
LN_EPS = 1e-5
RMS_EPS = 1e-6
DEPTH = 2
ALPHA = (2 * DEPTH) ** 0.25

COL_A_U, COL_A_V, COL_A_Z = 0, 1024, 2048
COL_B_Q, COL_B_K, COL_B_V, COL_B_Z = 3072, 4096, 5120, 6144
COL_C_QKV, COL_C_Z = 7168, 10240
N_MAIN = 11264
COL_SMALL = 11264
COL_GATE = COL_SMALL + 2 * H_DN
LOG2E = 1.4426950408889634


def _params(*sem):
    return pltpu.CompilerParams(dimension_semantics=sem, vmem_limit_bytes=VMEM_LIMIT_BYTES)


def _silu(x):
    return x * (1.0 / (1.0 + jnp.exp(-x)))


def _sigmoid(x):
    return 1.0 / (1.0 + jnp.exp(-x))


def _gelu(x):
    c = math.sqrt(2.0 / math.pi)
    return 0.5 * x * (1.0 + jnp.tanh(c * (x + 0.044715 * (x * x * x))))


def _layer_norm_rows(x, g, b):
    mu = jnp.mean(x, axis=-1, keepdims=True)
    xc = x - mu
    var = jnp.mean(xc * xc, axis=-1, keepdims=True)
    return xc * lax.rsqrt(var + LN_EPS) * g + b


def _ln_kernel(x_ref, g_ref, b_ref, o_ref, ob_ref):
    y = _layer_norm_rows(x_ref[...], g_ref[...], b_ref[...])
    o_ref[...] = y
    ob_ref[...] = y.astype(BF16)


def _layer_norm(x, g, b, tm=512):
    m, d = x.shape
    tm = min(tm, m)
    return pl.pallas_call(
        _ln_kernel,
        grid=(m // tm,),
        in_specs=[pl.BlockSpec((tm, d), lambda i: (i, 0)),
                  pl.BlockSpec((1, d), lambda i: (0, 0)),
                  pl.BlockSpec((1, d), lambda i: (0, 0))],
        out_specs=[pl.BlockSpec((tm, d), lambda i: (i, 0))] * 2,
        out_shape=[jax.ShapeDtypeStruct((m, d), F32), jax.ShapeDtypeStruct((m, d), BF16)],
        compiler_params=_params("arbitrary"),
        name="ln_in",
    )(x, g.reshape(1, d), b.reshape(1, d))


def _mm_kernel(x_ref, w_ref, o_ref, wb_ref):
    @pl.when(pl.program_id(1) == 0)
    def _():
        wb_ref[...] = w_ref[...].astype(BF16)

    o_ref[...] = lax.dot_general(x_ref[...], wb_ref[...], (((1,), (1,)), ((), ())),
                                 preferred_element_type=F32)


def _matmul(xb, w_t, layer, n, tm, tn, name):
    m, k = xb.shape
    tm = min(tm, m)
    return pl.pallas_call(
        _mm_kernel,
        grid=(n // tn, m // tm),
        in_specs=[pl.BlockSpec((tm, k), lambda j, i: (i, 0)),
                  pl.BlockSpec((None, tn, k), lambda j, i: (layer, j, 0))],
        out_specs=pl.BlockSpec((tm, tn), lambda j, i: (i, j)),
        out_shape=jax.ShapeDtypeStruct((m, n), F32),
        scratch_shapes=[pltpu.VMEM((tn, k), BF16)],
        compiler_params=_params("arbitrary", "arbitrary"),
        name=name,
    )(xb, w_t)


def _sg_kernel(u_ref, v_ref, z_ref, w_ref, bias_ref, g_ref, b_ref, o_ref, vn_ref):
    u = _gelu(u_ref[...])
    vn = _layer_norm_rows(_gelu(v_ref[...]), g_ref[...], b_ref[...])
    vn_ref[...] = vn
    vb = vn.astype(BF16)
    rows = vn.shape[0]
    r = lax.broadcasted_iota(jnp.int32, (rows, rows), 0)
    c = lax.broadcasted_iota(jnp.int32, (rows, rows), 1)
    lower = c <= r
    parts = []
    for grp in range(SG_GROUPS):
        w = jnp.where(lower, w_ref[grp], 0.0).astype(BF16)
        parts.append(jnp.dot(w, vb[:, grp * LANES:(grp + 1) * LANES], preferred_element_type=F32))
    mixed = jnp.concatenate(parts, axis=-1) + bias_ref[...]
    o_ref[...] = (u * mixed * _silu(z_ref[...])).astype(o_ref.dtype)


def _spatial_gating(proj, w_mix, bias_full, ln_g, ln_b):
    m = proj.shape[0]
    tm = SG_CHUNK
    blk = lambda col: pl.BlockSpec((tm, BRANCH_W), lambda i, col=col: (i, col // BRANCH_W))
    return pl.pallas_call(
        _sg_kernel,
        grid=(m // tm,),
        in_specs=[blk(COL_A_U), blk(COL_A_V), blk(COL_A_Z),
                  pl.BlockSpec((SG_GROUPS, tm, tm), lambda i: (0, 0, 0)),
                  pl.BlockSpec((tm, BRANCH_W), lambda i: (0, 0)),
                  pl.BlockSpec((1, BRANCH_W), lambda i: (0, 0)),
                  pl.BlockSpec((1, BRANCH_W), lambda i: (0, 0))],
        out_specs=[pl.BlockSpec((tm, BRANCH_W), lambda i: (i, 0)),
                   pl.BlockSpec((tm, BRANCH_W), lambda i: (i, 0))],
        out_shape=[jax.ShapeDtypeStruct((m, BRANCH_W), BF16),
                   jax.ShapeDtypeStruct((m, BRANCH_W), F32)],
        compiler_params=_params("arbitrary"),
        name="spatial_gating",
    )(proj, proj, proj, w_mix, bias_full, ln_g.reshape(1, -1), ln_b.reshape(1, -1))


def _merge_out_kernel(oa_ref, ob_ref, oc_ref, wb_ref, ga_ref, gb_ref, gc_ref, gbias_ref, wo_ref, x_ref,
                      g_ref, b_ref, o_ref, o_bf16_ref):
    merged = None
    for n, (o_n, gate_ref) in enumerate(((oa_ref, ga_ref), (ob_ref, gb_ref), (oc_ref, gc_ref))):
        y = jnp.dot(o_n[...].astype(BF16), wb_ref[n], preferred_element_type=F32)
        t = _sigmoid(gate_ref[...] + gbias_ref[n:n + 1, :]) * y
        merged = t if merged is None else merged + t
    out = jnp.dot(merged.astype(BF16), wo_ref[...], preferred_element_type=F32)
    x_new = _layer_norm_rows(ALPHA * x_ref[...] + out, g_ref[...], b_ref[...])
    o_ref[...] = x_new
    o_bf16_ref[...] = x_new.astype(BF16)


def _merge_out(o_a, o_b, o_c, gates, w_branch, gate_b, w_out, x, ln_g, ln_b, tm=256):
    m, d = x.shape
    tm = min(tm, m)
    resident = lambda shape: pl.BlockSpec(shape, lambda i: (0,) * len(shape), pipeline_mode=pl.Buffered(1))
    oblk = pl.BlockSpec((tm, BRANCH_W), lambda i: (i, 0))
    gblk = lambda n: pl.BlockSpec((tm, d), lambda i, n=n: (i, n))
    row = pl.BlockSpec((tm, d), lambda i: (i, 0))
    vec = pl.BlockSpec((1, d), lambda i: (0, 0))
    return pl.pallas_call(
        _merge_out_kernel,
        grid=(m // tm,),
        in_specs=[oblk, oblk, oblk, resident((N_BRANCH, BRANCH_W, d)), gblk(0), gblk(1), gblk(2),
                  pl.BlockSpec((N_BRANCH, d), lambda i: (0, 0)), resident((d, d)), row, vec, vec],
        out_specs=[row, row],
        out_shape=[jax.ShapeDtypeStruct((m, d), F32), jax.ShapeDtypeStruct((m, d), BF16)],
        compiler_params=_params("arbitrary"),
        name="merge_out",
    )(o_a, o_b, o_c, w_branch, gates, gates, gates, gate_b, w_out, x, ln_g.reshape(1, d), ln_b.reshape(1, d))


SB_SCALE = HD_SB ** -0.5 * LOG2E


def _split2(x):
    hi = x.astype(BF16)
    lo = (x - hi.astype(F32)).astype(BF16)
    return hi, lo


def _split3(x):
    hi = x.astype(BF16)
    r = x - hi.astype(F32)
    mid = r.astype(BF16)
    lo = (r - mid.astype(F32)).astype(BF16)
    return hi, mid, lo


def _suffix_matrix(bk):
    j = lax.broadcasted_iota(jnp.int32, (2 * bk, bk), 0) % bk
    s = lax.broadcasted_iota(jnp.int32, (2 * bk, bk), 1)
    return jnp.where(j > s, 1.0, 0.0).astype(BF16)


def _sb_weights(zs, spent, suffix, masks):
    sign = jnp.uint32(0x80000000)
    log_betas, costs, pieces = [], [], []
    for z, mask in zip(zs, masks):
        neg_abs = lax.bitcast_convert_type(lax.bitcast_convert_type(z, jnp.uint32) | sign, F32)
        cost = jnp.maximum(z, 0.0) + jnp.log2(1.0 + jnp.exp2(neg_abs))
        log_betas.append(z - cost)
        if mask is not None:
            cost = jnp.where(mask, cost, 0.0)
        costs.append(cost)
        pieces.append(jnp.concatenate(_split2(cost), axis=1))
    withins = [jnp.dot(piece, suffix, preferred_element_type=F32) for piece in pieces]
    weights = []
    for log_beta, cost, within, mask in zip(log_betas, costs, withins, masks):
        a = jnp.exp2(log_beta - (spent + within))
        weights.append(a if mask is None else jnp.where(mask, a, 0.0))
        spent = spent + jnp.sum(cost, axis=-1, keepdims=True)
    return weights, spent


def _sb_prompt_kernel(bias_ref, q_ref, k_ref, v_ref, zg_ref, o_ref, ko_ref, vo_ref, kb_ref, vb_ref, *, bq, bk,
                      unroll):
    h = pl.program_id(0)
    qi = pl.program_id(1)
    nsub = bq // bk

    @pl.when(qi == 0)
    def _():
        k = k_ref[...]
        v = v_ref[...]
        ko_ref[...] = k
        vo_ref[...] = v
        lane = lax.broadcasted_iota(jnp.int32, k.shape, 1)
        kb_ref[:, :HD_SB] = k.astype(BF16)
        kb_ref[:, HD_SB:] = jnp.where(lane < 3, 1.0, 0.0).astype(BF16)
        vb_ref[...] = v.astype(BF16)

    q = q_ref[...]
    lane = lax.broadcasted_iota(jnp.int32, q.shape, 1)
    b_hi, b_mid, b_lo = (piece.astype(F32) for piece in _split3(jnp.full(q.shape, bias_ref[h] * LOG2E, F32)))
    q_bias = jnp.where(lane == 0, b_hi, jnp.where(lane == 1, b_mid, jnp.where(lane == 2, b_lo, 0.0)))
    qb = jnp.concatenate([(q * SB_SCALE).astype(BF16), q_bias.astype(BF16)], axis=1)
    suffix = _suffix_matrix(bk)
    q_pos = qi * bq + lax.broadcasted_iota(jnp.int32, (bq, bk), 0)
    k_off = lax.broadcasted_iota(jnp.int32, (bq, bk), 1)

    def steps(js, carry, acc, masked):
        starts = [pl.multiple_of(j * bk, bk) for j in js]
        zs = [lax.dot_general(qb, kb_ref[pl.ds(st, bk), :], (((1,), (1,)), ((), ())),
                              preferred_element_type=F32) for st in starts]
        masks = [(j * bk + k_off) < q_pos if masked else None for j in js]
        weights, carry = _sb_weights(zs, carry, suffix, masks)
        for a, st in zip(weights, starts):
            acc = acc + jnp.dot(a.astype(BF16), vb_ref[pl.ds(st, bk), :], preferred_element_type=F32)
        return carry, acc

    carry = jnp.zeros((bq, 1), F32)
    acc = jnp.zeros((bq, HD_SB), F32)
    for t in range(nsub // unroll):
        first = qi * nsub + nsub - 1 - t * unroll
        carry, acc = steps([first - u for u in range(unroll)], carry, acc, True)

    def body(t, ca):
        first = qi * nsub - 1 - t * unroll
        return steps([first - u for u in range(unroll)], ca[0], ca[1], False)

    carry, acc = lax.fori_loop(0, qi * (nsub // unroll), body, (carry, acc))
    o_ref[...] = (acc * _silu(zg_ref[...])).astype(o_ref.dtype)


def _sb_prompt(proj, sb_bias, bq=1024, bk=256, unroll=2):
    m = proj.shape[0]
    assert (bq // bk) % unroll == 0
    cb = lambda col: col // HD_SB
    return pl.pallas_call(
        functools.partial(_sb_prompt_kernel, bq=bq, bk=bk, unroll=unroll),
        grid=(H_SB, m // bq),
        in_specs=[pl.BlockSpec(memory_space=pltpu.SMEM),
                  pl.BlockSpec((bq, HD_SB), lambda h, i: (i, cb(COL_B_Q) + h)),
                  pl.BlockSpec((m, HD_SB), lambda h, i: (0, cb(COL_B_K) + h)),
                  pl.BlockSpec((m, HD_SB), lambda h, i: (0, cb(COL_B_V) + h)),
                  pl.BlockSpec((bq, HD_SB), lambda h, i: (i, cb(COL_B_Z) + h))],
        out_specs=[pl.BlockSpec((bq, HD_SB), lambda h, i: (i, h)),
                   pl.BlockSpec((m, HD_SB), lambda h, i: (0, h)),
                   pl.BlockSpec((m, HD_SB), lambda h, i: (0, h))],
        out_shape=[jax.ShapeDtypeStruct((m, BRANCH_W), BF16),
                   jax.ShapeDtypeStruct((m, BRANCH_W), F32),
                   jax.ShapeDtypeStruct((m, BRANCH_W), F32)],
        scratch_shapes=[pltpu.VMEM((m, 2 * HD_SB), BF16), pltpu.VMEM((m, HD_SB), BF16)],
        compiler_params=_params("arbitrary", "arbitrary"),
        name="sb_prompt",
    )(sb_bias, proj, proj, proj, proj)


def _sb_sample_kernel(pt_ref, bias_ref, q_ref, kn_ref, vn_ref, zg_ref, *rest, dec_seq, pages_per_step,
                      page_group):
    del pt_ref
    npg = pages_per_step
    k_refs, v_refs = rest[:npg], rest[npg:2 * npg]
    o_ref, acc_ref, carry_ref = rest[2 * npg:]
    step = pl.program_id(1)
    bk = PAGE_SIZE
    rows = H_SB * dec_seq
    suffix = _suffix_matrix(bk)
    bias = bias_ref[...]
    head_rows = lambda x, h: x[h * dec_seq:(h + 1) * dec_seq]

    def attend(zs, get_vs, masks, carry, acc):
        weights, carry = _sb_weights(zs, carry, suffix, masks)
        for a, get_v in zip(weights, get_vs):
            outs = [jnp.dot(head_rows(a, h).astype(BF16), get_v(h).astype(BF16), preferred_element_type=F32)
                    for h in range(H_SB)]
            acc = acc + jnp.concatenate(outs, axis=0)
        return carry, acc

    @pl.when(step == 0)
    def _():
        pad = jnp.zeros((bk - dec_seq, HD_SB), F32)
        zs = []
        for h in range(H_SB):
            sl = slice(h * HD_SB, (h + 1) * HD_SB)
            kh = jnp.concatenate([kn_ref[:, sl], pad], axis=0).astype(BF16)
            qh = (q_ref[:, sl] * SB_SCALE).astype(BF16)
            zs.append(lax.dot_general(qh, kh, (((1,), (1,)), ((), ())), preferred_element_type=F32))
        new_v = lambda h: jnp.concatenate([vn_ref[:, h * HD_SB:(h + 1) * HD_SB], pad], axis=0)
        t = lax.broadcasted_iota(jnp.int32, (rows, bk), 0) % dec_seq
        pos = lax.broadcasted_iota(jnp.int32, (rows, bk), 1)
        carry, acc = attend([jnp.concatenate(zs, axis=0) + bias], [new_v], [pos < t],
                            jnp.zeros((rows, 1), F32), jnp.zeros((rows, HD_SB), F32))
        carry_ref[...] = carry
        acc_ref[...] = acc

    rid = lax.broadcasted_iota(jnp.int32, (bk, HD_SB), 0) // dec_seq
    q_rows = []
    for h in range(H_SB):
        qh = jnp.concatenate([q_ref[:, h * HD_SB:(h + 1) * HD_SB] * SB_SCALE] * (bk // dec_seq), axis=0)
        q_rows.append(jnp.where(rid == h, qh, 0.0).astype(BF16))

    def logits(pages):
        zp = None
        for h in range(H_SB):
            kh = jnp.concatenate([k_refs[p][pl.ds(h, bk, stride=H_SB), :] for p in pages], axis=0)
            part = lax.dot_general(kh.astype(BF16), q_rows[h], (((1,), (1,)), ((), ())),
                                   preferred_element_type=F32)
            zp = part if zp is None else zp + part
        return [zp[i * bk:(i + 1) * bk].T[:rows] + bias for i in range(len(pages))]

    groups = [list(range(g, g + page_group)) for g in range(0, npg, page_group)]
    carry, acc = carry_ref[...], acc_ref[...]
    zs_next = logits(groups[0])
    for gi, pages in enumerate(groups):
        zs = zs_next
        if gi + 1 < len(groups):
            zs_next = logits(groups[gi + 1])
        get_vs = [lambda h, p=p: v_refs[p][pl.ds(h, bk, stride=H_SB), :] for p in pages]
        carry, acc = attend(zs, get_vs, [None] * len(pages), carry, acc)
    carry_ref[...] = carry
    acc_ref[...] = acc

    @pl.when(step == pl.num_programs(1) - 1)
    def _():
        for h in range(H_SB):
            sl = slice(h * HD_SB, (h + 1) * HD_SB)
            o_ref[:, sl] = head_rows(acc_ref, h) * _silu(zg_ref[:, sl])


def _sb_sample(proj, cache_k, cache_v, page_table, sb_bias, layer, dec_seq, pages_per_step=16, page_group=4):
    m = proj.shape[0]
    nb = m // dec_seq
    n_pages = page_table.shape[1]
    npg = pages_per_step
    assert n_pages % npg == 0 and PAGE_SIZE % dec_seq == 0
    cb = lambda col: col // BRANCH_W
    page = lambda p: (lambda b, s, pt: (layer, pt[b, n_pages - 1 - (s * npg + p)], 0, 0))
    row = lambda col: pl.BlockSpec((dec_seq, BRANCH_W), lambda b, s, pt, col=col: (b, cb(col)))
    rows = H_SB * dec_seq
    bias_tile = jnp.broadcast_to(jnp.repeat(sb_bias * LOG2E, dec_seq)[:, None], (rows, PAGE_SIZE))
    page_spec = lambda p: pl.BlockSpec((None, None, PAGE_SIZE * H_SB, HD_SB), page(p))
    grid_spec = pltpu.PrefetchScalarGridSpec(
        num_scalar_prefetch=1,
        grid=(nb, n_pages // npg),
        in_specs=[pl.BlockSpec((rows, PAGE_SIZE), lambda b, s, pt: (0, 0)),
                  row(COL_B_Q), row(COL_B_K), row(COL_B_V), row(COL_B_Z)]
                 + [page_spec(p) for p in range(npg)] + [page_spec(p) for p in range(npg)],
        out_specs=pl.BlockSpec((dec_seq, BRANCH_W), lambda b, s, pt: (b, 0)),
        scratch_shapes=[pltpu.VMEM((rows, HD_SB), F32), pltpu.VMEM((rows, 1), F32)],
    )
    return pl.pallas_call(
        functools.partial(_sb_sample_kernel, dec_seq=dec_seq, pages_per_step=npg, page_group=page_group),
        grid_spec=grid_spec,
        out_shape=jax.ShapeDtypeStruct((m, BRANCH_W), F32),
        compiler_params=_params("arbitrary", "arbitrary"),
        name="sb_sample",
    )(page_table, bias_tile, proj, proj, proj, proj, *([cache_k] * npg), *([cache_v] * npg))


def _softplus(x):
    return jnp.maximum(x, 0.0) + jnp.log(1.0 + jnp.exp(-jnp.abs(x)))


def _dot_exact_rhs(a_f32, sel_bf16):
    hi, mid, lo = _split3(a_f32)
    d = lambda p: jnp.dot(p, sel_bf16, preferred_element_type=F32)
    return d(hi) + (d(mid) + d(lo))


def _dot_exact_lhs(sel_bf16, b_f32):
    hi, mid, lo = _split3(b_f32)
    d = lambda p: jnp.dot(sel_bf16, p, preferred_element_type=F32)
    return d(hi) + (d(mid) + d(lo))


def _dot_x3(a, b):
    ah, al = _split2(a)
    bh, bl = _split2(b)
    d = lambda x, y: jnp.dot(x, y, preferred_element_type=F32)
    return d(ah, bh) + (d(ah, bl) + d(al, bh))


def _dn_prep_tail(seg_inputs, w_ref, x_ref, ws_ref, alog_ref, dtb_ref, eg_ref, eb_ref,
                  qn_ref, kn_ref, vc_ref, g_ref, beta_ref):
    outs = (qn_ref, kn_ref, vc_ref)
    for seg, (x0, x1, x2, x3) in enumerate(seg_inputs):
        sl = slice(seg * BRANCH_W, (seg + 1) * BRANCH_W)
        w = w_ref[:, sl]
        y = _silu(((x3 * w[0:1] + x2 * w[1:2]) + x1 * w[2:3]) + x0 * w[3:4])
        if seg < 2:
            scale = DK_DN ** -0.5 if seg == 0 else 1.0
            for h in range(H_DN):
                hs = slice(h * DK_DN, (h + 1) * DK_DN)
                yh = y[:, hs]
                ss = jnp.sum(yh * yh, axis=-1, keepdims=True)
                outs[seg][:, hs] = yh * lax.rsqrt(ss + RMS_EPS) * scale
        else:
            outs[seg][...] = y
    ps = lax.dot_general(x_ref[...].astype(BF16), ws_ref[...].astype(BF16), (((1,), (1,)), ((), ())),
                         preferred_element_type=F32)
    g = -jnp.exp(alog_ref[...]) * _softplus(ps + dtb_ref[...])
    g_ref[...] = _dot_exact_rhs(g, eg_ref[...])
    beta_ref[...] = _dot_exact_rhs(_sigmoid(ps), eb_ref[...])


def _dn_prep_prompt_kernel(xq_ref, xk_ref, xv_ref, hq_ref, hk_ref, hv_ref, *rest):
    first = pl.program_id(0) == 0
    segs = []
    for x_ref, h_ref in ((xq_ref, hq_ref), (xk_ref, hk_ref), (xv_ref, hv_ref)):
        x = x_ref[...]
        halo = jnp.where(first, 0.0, h_ref[...])
        rid = lax.broadcasted_iota(jnp.int32, halo.shape, 0)
        shifted = []
        for k in (1, 2, 3):
            xr = pltpu.roll(x, k, 0)
            head = jnp.where(rid < k, pltpu.roll(halo, k, 0), xr[:8])
            shifted.append(jnp.concatenate([head, xr[8:]], axis=0))
        segs.append((x, *shifted))
    _dn_prep_tail(segs, *rest)


def _dn_prep_sample_kernel(xq_ref, xk_ref, xv_ref, *rest):
    shifted = rest[:9]
    segs = []
    for s, x_ref in enumerate((xq_ref, xk_ref, xv_ref)):
        segs.append((x_ref[...], shifted[s][...], shifted[3 + s][...], shifted[6 + s][...]))
    _dn_prep_tail(segs, *rest[9:])


def _dn_prep(proj, x, w_small_t, layer, shifted, conv_w, a_log, dt_bias, tm=256):
    m = proj.shape[0]
    cb = lambda col: col // BRANCH_W
    xblk = lambda s: pl.BlockSpec((tm, BRANCH_W), lambda i, s=s: (i, cb(COL_C_QKV) + s))
    in_specs = [xblk(0), xblk(1), xblk(2)]
    args = [proj, proj, proj]
    if shifted is None:
        hblk = lambda s: pl.BlockSpec(
            (8, BRANCH_W), lambda i, s=s: (jnp.maximum(i * (tm // 8) - 1, 0), cb(COL_C_QKV) + s))
        in_specs += [hblk(0), hblk(1), hblk(2)]
        args += [proj, proj, proj]
        body = _dn_prep_prompt_kernel
    else:
        for arr in shifted:
            for s in range(3):
                in_specs.append(pl.BlockSpec((tm, BRANCH_W), lambda i, s=s: (i, s)))
                args.append(arr)
        body = _dn_prep_sample_kernel
    lane = lax.broadcasted_iota(jnp.int32, (LANES, BRANCH_W), 0)
    head = lax.broadcasted_iota(jnp.int32, (LANES, BRANCH_W), 1) // DV_DN
    e_g = (lane == head).astype(BF16)
    e_b = (lane == head + H_DN).astype(BF16)
    pad = lambda v, off: jnp.zeros((1, LANES), F32).at[0, off:off + H_DN].set(v)
    const = lambda shape: pl.BlockSpec(shape, lambda i: (0, 0))
    in_specs += [const((CONV_W, D_DN_CONV)), pl.BlockSpec((tm, D_MODEL), lambda i: (i, 0)),
                 pl.BlockSpec((None, LANES, D_MODEL), lambda i: (layer, 0, 0)),
                 const((1, LANES)), const((1, LANES)), const((LANES, BRANCH_W)), const((LANES, BRANCH_W))]
    args += [conv_w, x, w_small_t, pad(a_log, 0), pad(dt_bias, 0), e_g, e_b]
    out = pl.BlockSpec((tm, BRANCH_W), lambda i: (i, 0))
    return pl.pallas_call(
        body,
        grid=(m // tm,),
        in_specs=in_specs,
        out_specs=[out] * 5,
        out_shape=[jax.ShapeDtypeStruct((m, BRANCH_W), F32)] * 5,
        compiler_params=_params("arbitrary"),
        name="dn_prep",
    )(*args)


def _dn_masks(c, block):
    i = lax.broadcasted_iota(jnp.int32, (c, c), 0)
    j = lax.broadcasted_iota(jnp.int32, (c, c), 1)
    same = (i // block) == (j // block)
    return dict(incl=same & (i >= j), strict=same & (i > j), incl_t=same & (i <= j), same=same,
                eye=(i == j))


def _dn_chunk_core(qs, ks, vs, g_reps, beta_reps, masks, n_double):
    c = qs[0].shape[0]
    each = lambda f, *lists: [f(*args) for args in zip(*lists)]
    onef = lambda m: jnp.where(m, 1.0, 0.0).astype(BF16)
    incl = masks["incl"]
    incl_b, same_b, ones_b = onef(incl), onef(masks["same"]), jnp.ones((c, c), BF16)
    contract_last = lambda x, y: lax.dot_general(x, y, (((1,), (1,)), ((), ())), preferred_element_type=F32)
    gcums = each(lambda g: _dot_exact_lhs(incl_b, g), g_reps)
    glasts = each(lambda g: _dot_exact_lhs(same_b, g), g_reps)
    g_rows = each(lambda g: _dot_exact_lhs(ones_b, jnp.where(masks["incl_t"], g[:, :c], 0.0)), g_reps)
    decays = each(lambda gc, gr: jnp.where(incl, jnp.exp(jnp.where(incl, gc[:, :c] - gr, 0.0)), 0.0),
                  gcums, g_rows)
    kbs = each(lambda k: k.astype(BF16), ks)
    kks = each(lambda kb: contract_last(kb, kb), kbs)
    qks = each(lambda q, kb, d: contract_last(q.astype(BF16), kb) * d, qs, kbs, decays)
    eye = jnp.where(masks["eye"], 1.0, 0.0)
    ps = each(lambda b, kk, d: jnp.where(masks["strict"], -(b[:, :c] * kk * d), 0.0), beta_reps, kks, decays)
    xs = each(lambda p: eye + p, ps)
    for _ in range(n_double):
        ps = each(lambda p: _dot_x3(p, p), ps)
        xs = each(lambda x, p: x + _dot_x3(x, p), xs, ps)
    gams = each(jnp.exp, gcums)
    u_parts = each(lambda x, b, v: _dot_x3(x, b * v), xs, beta_reps, vs)
    w_parts = each(lambda x, b, gam, k: _dot_x3(x, (b * gam) * k), xs, beta_reps, gams, ks)
    q_gs = each(lambda q, gam: q * gam, qs, gams)
    k_ends = each(lambda k, gl, gc: k * jnp.exp(gl - gc), ks, glasts, gcums)
    g_ends = each(jnp.exp, glasts)
    return u_parts, w_parts, qks, q_gs, k_ends, g_ends


def _gated_rms(o, norm_g, z):
    ms = jnp.mean(o * o, axis=-1, keepdims=True)
    return o * lax.rsqrt(ms + RMS_EPS) * norm_g * _silu(z)


def _dn_prompt_kernel(q_ref, k_ref, v_ref, g_ref, b_ref, z_ref, ng_ref, o_ref, sfin_ref, s_ref):
    n = pl.program_id(0)

    @pl.when(n == 0)
    def _():
        s_ref[...] = jnp.zeros_like(s_ref)

    c = q_ref.shape[0]
    masks = _dn_masks(c, c)
    heads = range(H_DN)
    cols = [slice(h * DV_DN, (h + 1) * DV_DN) for h in heads]
    take = lambda ref: [ref[:, hs] for hs in cols]
    u_ps, w_ps, qks, q_gs, k_ends, g_ends = _dn_chunk_core(
        take(q_ref), take(k_ref), take(v_ref), take(g_ref), take(b_ref), masks, int(math.log2(c)) - 1)
    dot = lambda x, y: jnp.dot(x, y, preferred_element_type=F32)
    ss = [s_ref[h] for h in heads]
    sbs = [s.astype(BF16) for s in ss]
    ubs = [(u_ps[h] - dot(w_ps[h].astype(BF16), sbs[h])).astype(BF16) for h in heads]
    os_ = [dot(q_gs[h].astype(BF16), sbs[h]) + dot(qks[h].astype(BF16), ubs[h]) for h in heads]
    for h in heads:
        s_ref[h] = g_ends[h][0:1, :] * ss[h] + lax.dot_general(
            k_ends[h].astype(BF16), ubs[h], (((0,), (0,)), ((), ())), preferred_element_type=F32)
        o_ref[:, cols[h]] = _gated_rms(os_[h], ng_ref[...], z_ref[:, cols[h]]).astype(o_ref.dtype)

    @pl.when(n == pl.num_programs(0) - 1)
    def _():
        sfin_ref[...] = s_ref[...]


def _dn_prompt(qn, kn, vc, g_rep, beta_rep, proj, norm_g):
    m = qn.shape[0]
    c = DN_CHUNK
    blk = pl.BlockSpec((c, BRANCH_W), lambda n: (n, 0))
    zblk = pl.BlockSpec((c, BRANCH_W), lambda n: (n, COL_C_Z // BRANCH_W))
    state = pl.BlockSpec((H_DN, DK_DN, DV_DN), lambda n: (0, 0, 0))
    return pl.pallas_call(
        _dn_prompt_kernel,
        grid=(m // c,),
        in_specs=[blk, blk, blk, blk, blk, zblk, pl.BlockSpec((1, DV_DN), lambda n: (0, 0))],
        out_specs=[blk, state],
        out_shape=[jax.ShapeDtypeStruct((m, BRANCH_W), BF16),
                   jax.ShapeDtypeStruct((H_DN, DK_DN, DV_DN), F32)],
        scratch_shapes=[pltpu.VMEM((H_DN, DK_DN, DV_DN), F32)],
        compiler_params=_params("arbitrary"),
        name="dn_prompt",
    )(qn, kn, vc, g_rep, beta_rep, proj, norm_g.reshape(1, DV_DN))


def _dn_sample_kernel(q_ref, k_ref, v_ref, g_ref, b_ref, z_ref, ng_ref, s0_ref, o_ref, snew_ref, *, dec_seq, n_seq):
    t = dec_seq
    rows = H_DN * t
    seqs, heads = range(n_seq), range(H_DN)
    cols = [slice(h * DV_DN, (h + 1) * DV_DN) for h in heads]
    srows = [slice(i * t, (i + 1) * t) for i in seqs]
    take = lambda ref: [jnp.concatenate([ref[srows[i], hs] for hs in cols], axis=0) for i in seqs]
    head_rows = lambda x, h: x[h * t:(h + 1) * t]
    masks = _dn_masks(rows, t)
    u_ps, w_ps, qks, q_gs, k_ends, g_ends = _dn_chunk_core(
        take(q_ref), take(k_ref), take(v_ref), take(g_ref), take(b_ref), masks, max(int(math.log2(t)) - 1, 0))
    dot = lambda x, y: jnp.dot(x, y, preferred_element_type=F32)
    sbs = [[s0_ref[i, h].astype(BF16) for h in heads] for i in seqs]
    ubs = [jnp.concatenate([head_rows(u_ps[i], h) - dot(head_rows(w_ps[i], h).astype(BF16), sbs[i][h])
                            for h in heads], axis=0).astype(BF16) for i in seqs]
    os_ = [jnp.concatenate([dot(head_rows(q_gs[i], h).astype(BF16), sbs[i][h]) for h in heads], axis=0)
           + dot(qks[i].astype(BF16), ubs[i]) for i in seqs]
    rid = lax.broadcasted_iota(jnp.int32, (rows, DK_DN), 0) // t
    for i in seqs:
        for h in heads:
            kh = jnp.where(rid == h, k_ends[i], 0.0).astype(BF16)
            upd = lax.dot_general(kh, ubs[i], (((0,), (0,)), ((), ())), preferred_element_type=F32)
            snew_ref[i, h] = head_rows(g_ends[i], h)[0:1, :] * s0_ref[i, h] + upd
            o_ref[srows[i], cols[h]] = _gated_rms(head_rows(os_[i], h), ng_ref[...],
                                                  z_ref[srows[i], cols[h]]).astype(o_ref.dtype)


def _dn_sample(qn, kn, vc, g_rep, beta_rep, proj, norm_g, s0, layer, dec_seq, n_seq=4):
    m = qn.shape[0]
    nb = m // dec_seq
    assert nb % n_seq == 0
    blk = pl.BlockSpec((n_seq * dec_seq, BRANCH_W), lambda b: (b, 0))
    zblk = pl.BlockSpec((n_seq * dec_seq, BRANCH_W), lambda b: (b, COL_C_Z // BRANCH_W))
    return pl.pallas_call(
        functools.partial(_dn_sample_kernel, dec_seq=dec_seq, n_seq=n_seq),
        grid=(nb // n_seq,),
        in_specs=[blk, blk, blk, blk, blk, zblk, pl.BlockSpec((1, DV_DN), lambda b: (0, 0)),
                  pl.BlockSpec((None, n_seq, H_DN, DK_DN, DV_DN), lambda b: (layer, b, 0, 0, 0))],
        out_specs=[blk, pl.BlockSpec((n_seq, H_DN, DK_DN, DV_DN), lambda b: (b, 0, 0, 0))],
        out_shape=[jax.ShapeDtypeStruct((m, BRANCH_W), BF16),
                   jax.ShapeDtypeStruct((nb, H_DN, DK_DN, DV_DN), F32)],
        compiler_params=_params("arbitrary"),
        name="dn_sample",
    )(qn, kn, vc, g_rep, beta_rep, proj, norm_g.reshape(1, DV_DN), s0)


def _mixer_layer(x, xb, p, layer, sample, tm):
    m = x.shape[0]
    proj = _matmul(xb, p["w_main"], layer, N_MAIN, tm, 1024, "proj_main")
    gates = _matmul(xb, p["w_gate"], layer, N_BRANCH * D_MODEL, tm, 1024, "proj_gate")

    if sample is None:
        w_mix, bias_full = p["sg_w"], p["sg_bias_prompt"]
    else:
        w_mix, bias_full = p["sg_w_sample"], p["sg_bias_sample"]
    o_a, sg_v = _spatial_gating(proj, w_mix, bias_full, p["sg_ln_g"], p["sg_ln_b"])

    if sample is None:
        o_b, k_rows, v_rows = _sb_prompt(proj, p["sb_bias"])
        qn, kn, vc, g_rep, beta_rep = _dn_prep(proj, x, p["w_small"], layer, None, p["dn_conv_w"],
                                               p["dn_a_log"], p["dn_dt_bias"])
        o_c, s_new = _dn_prompt(qn, kn, vc, g_rep, beta_rep, proj, p["dn_norm_g"])
        conv_new = proj[m - (CONV_W - 1):, COL_C_QKV:COL_C_QKV + D_DN_CONV][None]
        s_new = s_new[None]
    else:
        t = sample["dec_seq"]
        nb = m // t
        o_b = _sb_sample(proj, sample["cache_k"], sample["cache_v"], sample["page_table"], p["sb_bias"],
                         layer, t)
        k_rows = proj[:, COL_B_K:COL_B_K + BRANCH_W]
        v_rows = proj[:, COL_B_V:COL_B_V + BRANCH_W]
        conv_in = proj[:, COL_C_QKV:COL_C_QKV + D_DN_CONV]
        xp = jnp.concatenate([sample["conv"], conv_in.reshape(nb, t, D_DN_CONV)], axis=1)
        shifted = [xp[:, CONV_W - 1 - k:CONV_W - 1 - k + t].reshape(m, D_DN_CONV) for k in (1, 2, 3)]
        qn, kn, vc, g_rep, beta_rep = _dn_prep(proj, x, p["w_small"], layer, shifted, p["dn_conv_w"],
                                               p["dn_a_log"], p["dn_dt_bias"])
        o_c, s_new = _dn_sample(qn, kn, vc, g_rep, beta_rep, proj, p["dn_norm_g"], sample["ssm"], layer, t)
        conv_new = xp[:, t:]

    x_new, xb_new = _merge_out(o_a, o_b, o_c, gates, p["w_branch"], p["gate_b"], p["w_out"], x, p["ln_g"],
                               p["ln_b"])
    return x_new, xb_new, k_rows, v_rows, conv_new, s_new, sg_v


def kernel(x_prompt, x_sample, cache_sb_k, cache_sb_v, state_dn_conv, state_dn_ssm, page_table, ln_in_g, ln_in_b,
           w_in, sg_ln_g, sg_ln_b, sg_w, sg_b, sb_bias, dn_conv_w, dn_a_log, dn_dt_bias, dn_norm_g, gate_b,
           w_branch, w_out, ln_g, ln_b):
    bp, seq, d = x_prompt.shape
    bs, dec_seq, _ = x_sample.shape
    assert bp == 1, "prompt rows are treated as one sequence"
    depth = w_in.shape[0]
    n_pool = cache_sb_k.shape[1]

    xp, xpb = _layer_norm(x_prompt.reshape(bp * seq, d), ln_in_g, ln_in_b)
    xs, xsb = _layer_norm(x_sample.reshape(bs * dec_seq, d), ln_in_g, ln_in_b)
    cache_k = cache_sb_k.reshape(depth, n_pool, PAGE_SIZE * H_SB, HD_SB)
    cache_v = cache_sb_v.reshape(depth, n_pool, PAGE_SIZE * H_SB, HD_SB)

    w_in_t = jnp.swapaxes(w_in, 1, 2)
    w_main = w_in_t
    w_gate = w_in_t[:, COL_GATE:]
    w_small = jnp.pad(w_in_t[:, COL_SMALL:COL_GATE], ((0, 0), (0, LANES - 2 * H_DN), (0, 0)))
    w_branch_b = w_branch.astype(BF16)
    w_out_b = w_out.astype(BF16)

    reps = SG_CHUNK // dec_seq
    eye = jnp.eye(reps, dtype=F32)
    outs = [[] for _ in range(9)]
    for l in range(depth):
        sg_small = jnp.tril(sg_w[l][:, :dec_seq, :dec_seq])
        p = {
            "w_main": w_main, "w_gate": w_gate, "w_small": w_small,
            "sg_w": sg_w[l],
            "sg_bias_prompt": jnp.repeat(sg_b[l].T, BRANCH_W // SG_GROUPS, axis=1),
            "sg_w_sample": jax.vmap(lambda w: jnp.kron(eye, w))(sg_small),
            "sg_bias_sample": jnp.repeat(jnp.tile(sg_b[l][:, :dec_seq].T, (reps, 1)),
                                         BRANCH_W // SG_GROUPS, axis=1),
            "sg_ln_g": sg_ln_g[l], "sg_ln_b": sg_ln_b[l], "sb_bias": sb_bias[l],
            "dn_conv_w": dn_conv_w[l], "dn_a_log": dn_a_log[l], "dn_dt_bias": dn_dt_bias[l],
            "dn_norm_g": dn_norm_g[l], "gate_b": gate_b[l],
            "w_branch": w_branch_b[l], "w_out": w_out_b[l],
            "ln_g": ln_g[l], "ln_b": ln_b[l],
        }
        xp, xpb, kp, vp, cp, sp, _ = _mixer_layer(xp, xpb, p, l, None, tm=1024)
        sample = {"cache_k": cache_k, "cache_v": cache_v, "page_table": page_table,
                  "conv": state_dn_conv[l], "ssm": state_dn_ssm, "dec_seq": dec_seq}
        xs, xsb, ks, vs, cs, ss, sgv = _mixer_layer(xs, xsb, p, l, sample, tm=1024)
        hd = (H_SB, HD_SB)
        for lst, val in zip(outs, (kp.reshape(bp, seq, *hd), vp.reshape(bp, seq, *hd),
                                   ks.reshape(bs, dec_seq, *hd), vs.reshape(bs, dec_seq, *hd),
                                   cp, cs, sp, ss, sgv.reshape(bs, dec_seq, BRANCH_W))):
            lst.append(val)
    return (xp.reshape(bp, seq, d), xs.reshape(bs, dec_seq, d), *[jnp.stack(o) for o in outs])
```
